```python
import jax, jax.numpy as jnp
from jax import lax
import numpy as np

D_MODEL = 1024
BATCH = 2
SEQ = 8192
DEPTH = 4
DEC_BATCH = 128
DEC_SEQ = 1
PAST_LEN = 2048
PAGE_SIZE = 128

D_RNN = D_MODEL
RNN_BLOCKS = 8
RNN_CONV = 4
LRU_C = 8.0
ATT_GROUPS = ((128, 1), (512, 4), (2048, 16))
N_GROUPS = len(ATT_GROUPS)
H_G = 4
HD = 128
ATT_W = H_G * HD
ATT_IN = N_GROUPS * 3 * ATT_W
D_CONV = D_MODEL
CCONV = 31
D_FF = 3 * D_MODEL
FFN_CONV = 3
N_BRANCH = 3
N_IN = D_RNN + ATT_IN + 2 * D_CONV + N_BRANCH * D_MODEL
EPS = 1e-6

kernel_name = 'hybrid_rglru_dilated_conformer_decoder_step'


def _rmsnorm(x, g):
    xf = x.astype(jnp.float32)
    y = xf * lax.rsqrt(jnp.mean(xf * xf, axis=-1, keepdims=True) + EPS)
    return (y * g.astype(jnp.float32)).astype(x.dtype)


def _layernorm(x, g, b):
    xf = x.astype(jnp.float32)
    mu = jnp.mean(xf, axis=-1, keepdims=True)
    xc = xf - mu
    y = xc * lax.rsqrt(jnp.mean(xc * xc, axis=-1, keepdims=True) + EPS)
    return (y * g.astype(jnp.float32) + b.astype(jnp.float32)).astype(x.dtype)


def _causal_dwconv(x, buf, w, b):
    wd, c = w.shape
    ext = jnp.concatenate([buf.astype(x.dtype), x], axis=1)
    y = lax.conv_general_dilated(ext, w.astype(x.dtype)[:, None, :], window_strides=(1,),
                                 padding='VALID', dimension_numbers=('NWC', 'WIO', 'NWC'),
                                 feature_group_count=c)
    return y + b.astype(x.dtype), ext[:, -(wd - 1):]


def _rglru(x, h0, wa, ba, wx, bx, lam):
    n, t, c = x.shape
    xb = x.reshape(n, t, RNN_BLOCKS, c // RNN_BLOCKS)
    r = jax.nn.sigmoid(jnp.einsum('ntbi,bij->ntbj', xb, wa).reshape(n, t, c) + ba)
    i = jax.nn.sigmoid(jnp.einsum('ntbi,bij->ntbj', xb, wx).reshape(n, t, c) + bx)
    log_a = (-LRU_C * r.astype(jnp.float32)) * jax.nn.softplus(-lam.astype(jnp.float32))
    a = jnp.exp(log_a)
    u = jnp.sqrt(-jnp.expm1(2.0 * log_a)) * (i * x).astype(jnp.float32)
    u = u.at[:, 0].add(a[:, 0] * h0.astype(jnp.float32))

    def combine(left, right):
        a1, b1 = left
        a2, b2 = right
        return a1 * a2, a2 * b1 + b2

    _, h = lax.associative_scan(combine, (a, u), axis=1)
    h = h.astype(x.dtype)
    return h, h[:, -1]


def _dilated_prompt(q, k, v, win, dil):
    n, s, h, e = q.shape
    kmax = win // dil
    blk = kmax
    m = -(-s // (dil * blk)) * blk
    sp = m * dil
    nb = m // blk

    def to_blocks(z):
        z = jnp.pad(z, ((0, 0), (0, sp - s), (0, 0), (0, 0)))
        z = z.reshape(n, m, dil, h, e).transpose(0, 2, 1, 3, 4)
        return z.reshape(n, dil, nb, blk, h, e)

    def with_prev(z):
        prev = jnp.pad(z, ((0, 0), (0, 0), (1, 0), (0, 0), (0, 0), (0, 0)))[:, :, :-1]
        return jnp.concatenate([prev, z], axis=3)

    qb = to_blocks(q)
    kk = with_prev(to_blocks(k))
    vv = with_prev(to_blocks(v))
    sc = jnp.einsum('nrbqhe,nrbkhe->nrbhqk', qb, kk,
                    preferred_element_type=jnp.float32) * (e ** -0.5)
    qi = jnp.arange(blk)[:, None]
    kj = jnp.arange(2 * blk)[None, :]
    dist = qi + blk - kj
    band = (dist >= 0) & (dist <= kmax)
    real = (jnp.arange(nb)[:, None, None] > 0) | (kj >= blk)[None]
    mask = band[None] & real
    sc = jnp.where(mask[:, None], sc, -jnp.inf)
    lse = jax.nn.logsumexp(sc, axis=-1)
    p = jnp.exp(sc - lse[..., None])
    o = jnp.einsum('nrbhqk,nrbkhe->nrbqhe', p, vv.astype(jnp.float32))
    o = o.reshape(n, dil, m, h, e).transpose(0, 2, 1, 3, 4).reshape(n, sp, h, e)[:, :s]
    lse = lse.transpose(0, 1, 2, 4, 3).reshape(n, dil, m, h).transpose(0, 2, 1, 3).reshape(n, sp, h)[:, :s]
    return o, lse


def _dilated_step(q, k, v, buf, win, dil):
    n, t, h, e = q.shape
    L = buf.shape[1]
    kmax = win // dil
    idx = L + jnp.arange(t)[:, None] - dil * jnp.arange(kmax + 1)[None, :]
    valid = idx >= 0
    from_buf = idx < L
    rows = jnp.stack([k, v], axis=2)
    g_buf = jnp.take(buf.astype(q.dtype), jnp.clip(idx, 0, L - 1), axis=1)
    g_new = jnp.take(rows, jnp.clip(idx - L, 0, t - 1), axis=1)
    g = jnp.where(from_buf[None, :, :, None, None, None], g_buf, g_new)
    sc = jnp.einsum('nthe,ntkhe->nthk', q, g[:, :, :, 0],
                    preferred_element_type=jnp.float32) * (e ** -0.5)
    sc = jnp.where(valid[None, :, None, :], sc, -jnp.inf)
    lse = jax.nn.logsumexp(sc, axis=-1)
    p = jnp.exp(sc - lse[..., None])
    o = jnp.einsum('nthk,ntkhe->nthe', p, g[:, :, :, 1].astype(jnp.float32))
    return o, lse, rows


def _layer(x, lw, rnn_h0, rnn_buf, cconv_buf, ffn_buf, kv_bufs):
    n, t, _ = x.shape
    xn = _rmsnorm(x, lw['norm1_g'])
    z = jnp.einsum('ntd,dc->ntc', xn, lw['w_in'])
    o1 = D_RNN
    o2 = o1 + ATT_IN
    o3 = o2 + 2 * D_CONV
    x_rnn = z[..., :o1]
    qkv = z[..., o1:o2].reshape(n, t, N_GROUPS, 3, H_G, HD)
    glu = z[..., o2:o3]
    gates = jax.nn.sigmoid(z[..., o3:].reshape(n, t, N_BRANCH, D_MODEL))

    u, rnn_buf_new = _causal_dwconv(x_rnn, rnn_buf, lw['rnn_conv_w'], lw['rnn_conv_b'])
    h_seq, h_last = _rglru(u, rnn_h0, lw['rnn_wa'], lw['rnn_ba'], lw['rnn_wx'], lw['rnn_bx'],
                           lw['rnn_lambda'])

    q = _rmsnorm(qkv[:, :, :, 0], lw['q_norm_g'][:, None, :])
    k = _rmsnorm(qkv[:, :, :, 1], lw['k_norm_g'][:, None, :])
    v = qkv[:, :, :, 2]
    outs, lses, kv_new = [], [], []
    for gi, (win, dil) in enumerate(ATT_GROUPS):
        qg, kg, vg = q[:, :, gi], k[:, :, gi], v[:, :, gi]
        if kv_bufs is None:
            og, lg = _dilated_prompt(qg, kg, vg, win, dil)
            keep = min(win, t)
            kv_new.append(jnp.stack([kg[:, t - keep:], vg[:, t - keep:]], axis=2))
        else:
            og, lg, rows = _dilated_step(qg, kg, vg, kv_bufs[gi], win, dil)
            kv_new.append(rows)
        outs.append(og)
        lses.append(lg)
    wg = jax.nn.softmax(jnp.stack(lses, 0), axis=0)
    att = jnp.einsum('gnth,gnthe->nthe', wg, jnp.stack(outs, 0)).astype(x.dtype).reshape(n, t, ATT_W)

    ga, gb = jnp.split(glu, 2, axis=-1)
    c_in = ga * jax.nn.sigmoid(gb)
    c, cconv_new = _causal_dwconv(c_in, cconv_buf, lw['cconv_w'], lw['cconv_b'])
    c = jax.nn.silu(_layernorm(c, lw['cnorm_g'], lw['cnorm_b']))

    merged = (gates[:, :, 0] * (h_seq @ lw['w_br_rnn'])
              + gates[:, :, 1] * (att @ lw['w_br_attn'])
              + gates[:, :, 2] * (c @ lw['w_br_conv']))
    x = x + merged @ lw['w_o']

    up = _rmsnorm(x, lw['norm2_g']) @ lw['ffn_up']
    up_c, ffn_new = _causal_dwconv(up, ffn_buf, lw['ffn_conv_w'], lw['ffn_conv_b'])
    fa, fb = jnp.split(up_c, 2, axis=-1)
    x = x + (jax.nn.gelu(fa) * fb) @ lw['ffn_down']
    return x, kv_new, h_last, rnn_buf_new, cconv_new, ffn_new


def setup_inputs(seed: int = 0) -> dict:
    key = jax.random.key(seed)
    ks = iter(jax.random.split(key, 48))
    f32 = jnp.float32

    def nrm(shape, scale):
        return jax.random.normal(next(ks), shape, f32) * scale

    def gain(shape):
        return 1.0 + nrm(shape, 0.01)

    bw = D_RNN // RNN_BLOCKS
    inp = {}
    inp['x_prompt'] = nrm((BATCH, SEQ, D_MODEL), 1.0)
    inp['x_sample'] = nrm((DEC_BATCH, DEC_SEQ, D_MODEL), 1.0)
    names = ['cache_kv_w128', 'cache_kv_w512', 'cache_kv_w2048']
    for nm, (win, dil) in zip(names, ATT_GROUPS):
        inp[nm] = nrm((DEPTH, DEC_BATCH, min(win, PAST_LEN), 2, H_G, HD), 1.0)
    inp['state_rnn_h'] = nrm((DEPTH, DEC_BATCH, D_RNN), 0.5)
    inp['state_rnn_conv'] = nrm((DEPTH, DEC_BATCH, RNN_CONV - 1, D_RNN), 1.0)
    inp['state_cconv'] = nrm((DEPTH, DEC_BATCH, CCONV - 1, D_CONV), 0.5)
    inp['state_ffn_conv'] = nrm((DEPTH, DEC_BATCH, FFN_CONV - 1, 2 * D_FF), 1.0)
    inp['norm1_g'] = gain((DEPTH, D_MODEL))
    inp['w_in'] = nrm((DEPTH, D_MODEL, N_IN), D_MODEL ** -0.5)
    inp['rnn_conv_w'] = nrm((DEPTH, RNN_CONV, D_RNN), RNN_CONV ** -0.5)
    inp['rnn_conv_b'] = nrm((DEPTH, D_RNN), 0.01)
    inp['rnn_wa'] = nrm((DEPTH, RNN_BLOCKS, bw, bw), bw ** -0.5)
    inp['rnn_ba'] = nrm((DEPTH, D_RNN), 0.01)
    inp['rnn_wx'] = nrm((DEPTH, RNN_BLOCKS, bw, bw), bw ** -0.5)
    inp['rnn_bx'] = nrm((DEPTH, D_RNN), 0.01)
    a0 = jax.random.uniform(next(ks), (DEPTH, D_RNN), f32, minval=0.9, maxval=0.999)
    sa = a0 ** (1.0 / LRU_C)
    inp['rnn_lambda'] = jnp.log(sa) - jnp.log1p(-sa)
    inp['q_norm_g'] = gain((DEPTH, N_GROUPS, HD))
    inp['k_norm_g'] = gain((DEPTH, N_GROUPS, HD))
    inp['cconv_w'] = nrm((DEPTH, CCONV, D_CONV), CCONV ** -0.5)
    inp['cconv_b'] = nrm((DEPTH, D_CONV), 0.01)
    inp['cnorm_g'] = gain((DEPTH, D_CONV))
    inp['cnorm_b'] = nrm((DEPTH, D_CONV), 0.01)
    inp['w_br_rnn'] = nrm((DEPTH, D_RNN, D_MODEL), D_RNN ** -0.5)
    inp['w_br_attn'] = nrm((DEPTH, ATT_W, D_MODEL), ATT_W ** -0.5)
    inp['w_br_conv'] = nrm((DEPTH, D_CONV, D_MODEL), D_CONV ** -0.5)
    inp['w_o'] = nrm((DEPTH, D_MODEL, D_MODEL), D_MODEL ** -0.5)
    inp['norm2_g'] = gain((DEPTH, D_MODEL))
    inp['ffn_up'] = nrm((DEPTH, D_MODEL, 2 * D_FF), D_MODEL ** -0.5)
    inp['ffn_conv_w'] = nrm((DEPTH, FFN_CONV, 2 * D_FF), FFN_CONV ** -0.5)
    inp['ffn_conv_b'] = nrm((DEPTH, 2 * D_FF), 0.01)
    inp['ffn_down'] = nrm((DEPTH, D_FF, D_MODEL), D_FF ** -0.5)
    return inp


def reference(x_prompt, x_sample, cache_kv_w128, cache_kv_w512, cache_kv_w2048,
              state_rnn_h, state_rnn_conv, state_cconv, state_ffn_conv,
              norm1_g, w_in, rnn_conv_w, rnn_conv_b, rnn_wa, rnn_ba, rnn_wx, rnn_bx, rnn_lambda,
              q_norm_g, k_norm_g, cconv_w, cconv_b, cnorm_g, cnorm_b,
              w_br_rnn, w_br_attn, w_br_conv, w_o, norm2_g, ffn_up, ffn_conv_w, ffn_conv_b,
              ffn_down):
    nbp = x_prompt.shape[0]
    dt = x_prompt.dtype
    p_h0 = jnp.zeros((nbp, D_RNN), dt)
    p_rb0 = jnp.zeros((nbp, RNN_CONV - 1, D_RNN), dt)
    p_cb0 = jnp.zeros((nbp, CCONV - 1, D_CONV), dt)
    p_fb0 = jnp.zeros((nbp, FFN_CONV - 1, 2 * D_FF), dt)
    xp, xs = x_prompt, x_sample
    pk = [[], [], []]
    sk = [[], [], []]
    ph, prc, pcc, pfc = [], [], [], []
    sh, src, scc, sfc = [], [], [], []
    for l in range(DEPTH):
        lw = {'norm1_g': norm1_g[l], 'w_in': w_in[l], 'rnn_conv_w': rnn_conv_w[l],
              'rnn_conv_b': rnn_conv_b[l], 'rnn_wa': rnn_wa[l], 'rnn_ba': rnn_ba[l],
              'rnn_wx': rnn_wx[l], 'rnn_bx': rnn_bx[l], 'rnn_lambda': rnn_lambda[l],
              'q_norm_g': q_norm_g[l], 'k_norm_g': k_norm_g[l], 'cconv_w': cconv_w[l],
              'cconv_b': cconv_b[l], 'cnorm_g': cnorm_g[l], 'cnorm_b': cnorm_b[l],
              'w_br_rnn': w_br_rnn[l], 'w_br_attn': w_br_attn[l], 'w_br_conv': w_br_conv[l],
              'w_o': w_o[l], 'norm2_g': norm2_g[l], 'ffn_up': ffn_up[l],
              'ffn_conv_w': ffn_conv_w[l], 'ffn_conv_b': ffn_conv_b[l], 'ffn_down': ffn_down[l]}
        xp, kvp, hp, rbp, cbp, fbp = _layer(xp, lw, p_h0, p_rb0, p_cb0, p_fb0, None)
        xs, kvs, hs, rbs, cbs, fbs = _layer(
            xs, lw, state_rnn_h[l], state_rnn_conv[l], state_cconv[l], state_ffn_conv[l],
            (cache_kv_w128[l], cache_kv_w512[l], cache_kv_w2048[l]))
        for gi in range(N_GROUPS):
            pk[gi].append(kvp[gi])
            sk[gi].append(kvs[gi])
        ph.append(hp)
        prc.append(rbp)
        pcc.append(cbp)
        pfc.append(fbp)
        sh.append(hs)
        src.append(rbs)
        scc.append(cbs)
        sfc.append(fbs)
    y_prompt, y_sample = xp, xs
    p_kv128, p_kv512, p_kv2048 = jnp.stack(pk[0], 0), jnp.stack(pk[1], 0), jnp.stack(pk[2], 0)
    s_kv128, s_kv512, s_kv2048 = jnp.stack(sk[0], 0), jnp.stack(sk[1], 0), jnp.stack(sk[2], 0)
    p_rnn_h, p_rnn_conv = jnp.stack(ph, 0), jnp.stack(prc, 0)
    p_cconv, p_ffn_conv = jnp.stack(pcc, 0), jnp.stack(pfc, 0)
    s_rnn_h, s_rnn_conv = jnp.stack(sh, 0), jnp.stack(src, 0)
    s_cconv, s_ffn_conv = jnp.stack(scc, 0), jnp.stack(sfc, 0)
    return (y_prompt, y_sample, p_kv128, p_kv512, p_kv2048, p_rnn_h, p_rnn_conv, p_cconv, p_ffn_conv,
            s_kv128, s_kv512, s_kv2048, s_rnn_h, s_rnn_conv, s_cconv, s_ffn_conv)
```

```python
import functools

import jax
import jax.numpy as jnp
from jax import lax
from jax.experimental import pallas as pl
from jax.experimental.pallas import tpu as pltpu

F32 = jnp.float32
BF16 = jnp.bfloat16

D_MODEL = 1024
D_RNN = D_MODEL
RNN_BLOCKS = 8
RNN_BW = D_RNN // RNN_BLOCKS
RNN_CONV = 4
LRU_C = 8.0
ATT_GROUPS = ((128, 1), (512, 4), (2048, 16))
N_GROUPS = len(ATT_GROUPS)
H_G = 4
HD = 128
ATT_W = H_G * HD
ATT_BLK = 128
D_CONV = D_MODEL
CCONV = 31
D_FF = 3 * D_MODEL
FFN_CONV = 3
N_BRANCH = 3
EPS = 1e-6
NEG = -1e30

COL_GLU = 0
COL_RNN = 2 * D_CONV
COL_GATE = COL_RNN + D_RNN
COL_QKV = COL_GATE + N_BRANCH * D_MODEL
N_IN = COL_QKV + N_GROUPS * 3 * ATT_W

VMEM_LIMIT = 56 * 1024 * 1024


def _cparams(sem):
    return pltpu.CompilerParams(dimension_semantics=sem, vmem_limit_bytes=VMEM_LIMIT)


def _sigmoid(x):
    return 1.0 / (1.0 + jnp.exp(-x))


def _softplus(x):
    return jnp.maximum(x, 0.0) + jnp.log1p(jnp.exp(-jnp.abs(x)))


def _rmsnorm_rows(x, g):
    ms = jnp.mean(x * x, axis=-1, keepdims=True)
    return x * lax.rsqrt(ms + EPS) * g


def _layernorm_swish(y, g, b):
    mu = jnp.mean(y, axis=-1, keepdims=True)
    yc = y - mu
    var = jnp.mean(yc * yc, axis=-1, keepdims=True)
    yn = yc * lax.rsqrt(var + EPS) * g + b
    return yn * _sigmoid(yn)


def _inproj_kernel(x_ref, g_ref, w_ref, o_ref, qkv_ref, xn_ref, *, n_main):
    j = pl.program_id(1)

    @pl.when(j == 0)
    def _():
        xn_ref[...] = _rmsnorm_rows(x_ref[...], g_ref[...]).astype(BF16)

    res = jnp.dot(xn_ref[...], w_ref[...], preferred_element_type=F32)

    @pl.when(j < n_main)
    def _():
        o_ref[...] = res

    @pl.when(j >= n_main)
    def _():
        for c in range(qkv_ref.shape[0]):
            qkv_ref[c] = res[:, c * HD:(c + 1) * HD]


def _inproj(x2d, g, w, tm, tn):
    t, d = x2d.shape
    n_main = COL_QKV // tn
    n_qkv = (N_IN - COL_QKV) // tn
    hpt = tn // HD
    return pl.pallas_call(
        functools.partial(_inproj_kernel, n_main=n_main),
        grid=(t // tm, n_main + n_qkv),
        in_specs=[pl.BlockSpec((tm, d), lambda i, j: (i, 0)),
                  pl.BlockSpec((1, d), lambda i, j: (0, 0)),
                  pl.BlockSpec((d, tn), lambda i, j: (0, j))],
        out_specs=[pl.BlockSpec((tm, tn), lambda i, j: (i, jnp.minimum(j, n_main - 1))),
                   pl.BlockSpec((hpt, tm, HD), lambda i, j: (jnp.maximum(j - n_main, 0), i, 0))],
        out_shape=[jax.ShapeDtypeStruct((t, COL_QKV), F32),
                   jax.ShapeDtypeStruct((n_qkv * hpt, t, HD), F32)],
        scratch_shapes=[pltpu.VMEM((tm, d), BF16)],
        compiler_params=_cparams(("parallel", "arbitrary")),
        name="inproj",
    )(x2d, g, w)


def _lru_coeffs(u, r_pre, i_pre, ba, bx, sp):
    r = _sigmoid(r_pre + ba)
    ig = _sigmoid(i_pre + bx)
    log_a = (-LRU_C * r) * sp
    a = jnp.exp(log_a)
    mult = jnp.sqrt(-jnp.tanh(log_a) * (a * a + 1.0))
    return a, mult * (ig * u)


RNN_RC = 256


def _rnn_prompt_kernel(x_ref, cw_ref, cb_ref, wa_ref, wx_ref, ba_ref, bx_ref, lam_ref,
                       h_ref, hlast_ref, ext_ref, a_ref, u_ref, cin_ref, hcar_ref, *, tm):
    i = pl.program_id(1)
    lc = tm // 8

    @pl.when(i == 0)
    def _():
        ext_ref[0:8, :] = jnp.zeros((8, D_RNN), F32)
        hcar_ref[...] = jnp.zeros((1, D_RNN), F32)

    ext_ref[8:8 + tm, :] = x_ref[...]

    for b in range(RNN_BLOCKS):
        cs = slice(b * RNN_BW, (b + 1) * RNN_BW)
        sp = _softplus(-lam_ref[:, cs])
        wa = wa_ref[b]
        wx = wx_ref[b]
        for c in range(tm // RNN_RC):
            r0 = c * RNN_RC
            u = cb_ref[:, cs]
            for k in range(RNN_CONV):
                lo = r0 + 8 - (RNN_CONV - 1) + k
                u = u + cw_ref[k:k + 1, cs] * ext_ref[lo:lo + RNN_RC, cs]
            ub = u.astype(BF16)
            r_pre = jnp.dot(ub, wa, preferred_element_type=F32)
            i_pre = jnp.dot(ub, wx, preferred_element_type=F32)
            a, uu = _lru_coeffs(u, r_pre, i_pre, ba_ref[:, cs], bx_ref[:, cs], sp)
            a_ref[b, r0:r0 + RNN_RC, :] = a
            u_ref[b, r0:r0 + RNN_RC, :] = uu

    ext_ref[0:8, :] = ext_ref[tm:tm + 8, :]

    def step(s, carry):
        hs, ps = carry
        rs = pl.ds(s, 8, stride=lc)
        h_new, p_new = [], []
        for b in range(RNN_BLOCKS):
            a = a_ref[b, rs, :]
            h = a * hs[b] + u_ref[b, rs, :]
            p = a * ps[b]
            u_ref[b, rs, :] = h
            a_ref[b, rs, :] = p
            h_new.append(h)
            p_new.append(p)
        return tuple(h_new), tuple(p_new)

    h_end, p_end = lax.fori_loop(
        0, lc, step, (tuple(jnp.zeros((8, RNN_BW), F32) for _ in range(RNN_BLOCKS)),
                      tuple(jnp.ones((8, RNN_BW), F32) for _ in range(RNN_BLOCKS))))

    for b in range(RNN_BLOCKS):
        cs = slice(b * RNN_BW, (b + 1) * RNN_BW)
        car = hcar_ref[:, cs]
        for c in range(8):
            cin_ref[c:c + 1, cs] = car
            car = h_end[b][c:c + 1, :] + p_end[b][c:c + 1, :] * car
        hcar_ref[:, cs] = car
        hlast_ref[:, cs] = car

    for b in range(RNN_BLOCKS):
        cs = slice(b * RNN_BW, (b + 1) * RNN_BW)
        for c in range(8):
            rows = slice(c * lc, (c + 1) * lc)
            h_ref[rows, cs] = (u_ref[b, rows, :]
                               + a_ref[b, rows, :] * cin_ref[c:c + 1, cs]).astype(h_ref.dtype)


def _rnn_prompt(z, cw, cb, wa, wx, ba, bx, lam, tm):
    n, s, _ = z.shape
    assert tm % RNN_RC == 0
    cblk = COL_RNN // D_RNN
    vec = pl.BlockSpec((1, D_RNN), lambda b, i: (0, 0))
    blkw = pl.BlockSpec((RNN_BLOCKS, RNN_BW, RNN_BW), lambda b, i: (0, 0, 0))
    return pl.pallas_call(
        functools.partial(_rnn_prompt_kernel, tm=tm),
        grid=(n, s // tm),
        in_specs=[pl.BlockSpec((None, tm, D_RNN), lambda b, i: (b, i, cblk)),
                  pl.BlockSpec((RNN_CONV, D_RNN), lambda b, i: (0, 0)),
                  vec, blkw, blkw, vec, vec, vec],
        out_specs=[pl.BlockSpec((None, tm, D_RNN), lambda b, i: (b, i, 0)),
                   pl.BlockSpec((None, 1, D_RNN), lambda b, i: (b, 0, 0))],
        out_shape=[jax.ShapeDtypeStruct((n, s, D_RNN), BF16),
                   jax.ShapeDtypeStruct((n, 1, D_RNN), F32)],
        scratch_shapes=[pltpu.VMEM((tm + 8, D_RNN), F32),
                        pltpu.VMEM((RNN_BLOCKS, tm, RNN_BW), F32),
                        pltpu.VMEM((RNN_BLOCKS, tm, RNN_BW), F32),
                        pltpu.VMEM((8, D_RNN), F32),
                        pltpu.VMEM((1, D_RNN), F32)],
        compiler_params=_cparams(("parallel", "arbitrary")),
        name="rnn_prompt",
    )(z, cw, cb, wa, wx, ba, bx, lam)


def _attn_prompt_kernel(q_ref, k_ref, v_ref, gq_ref, gk_ref,
                        o_ref, lse_ref, kn_ref, kprev_ref, vprev_ref, *, dil, tt):
    ti = pl.program_id(1)
    nblk = tt // (ATT_BLK * dil)
    units = nblk * dil
    sh = dil.bit_length() - 1

    @pl.when(ti == 0)
    def _():
        kprev_ref[...] = jnp.zeros(kprev_ref.shape, BF16)
        vprev_ref[...] = jnp.zeros(vprev_ref.shape, BF16)

    qi = lax.broadcasted_iota(jnp.int32, (ATT_BLK, ATT_BLK), 0)
    kj = lax.broadcasted_iota(jnp.int32, (ATT_BLK, ATT_BLK), 1)
    own_ok = kj <= qi
    prev_band = kj >= qi
    lane = lax.broadcasted_iota(jnp.int32, (ATT_BLK, HD), 1)
    gq = gq_ref[...]
    gk = gk_ref[...]
    scale = HD ** -0.5
    dn = (((1,), (1,)), ((), ()))

    def unit(u, carry):
        jb = u >> sh
        r = u & (dil - 1)
        start = jb * (ATT_BLK * dil) + r
        rs = pl.ds(start, ATT_BLK) if dil == 1 else pl.ds(start, ATT_BLK, stride=dil)
        prev_ok = jnp.logical_and(prev_band, (ti * nblk + jb) > 0)
        lse_tile = jnp.zeros((ATT_BLK, HD), F32)
        for h in range(H_G):
            qh = _rmsnorm_rows(q_ref[h, rs, :], gq).astype(BF16)
            kn = _rmsnorm_rows(k_ref[h, rs, :], gk)
            kn_ref[h, rs, :] = kn
            kb = kn.astype(BF16)
            vb = v_ref[h, rs, :].astype(BF16)
            kp = kprev_ref[r, h]
            vp = vprev_ref[r, h]
            s_own = lax.dot_general(qh, kb, dn, preferred_element_type=F32) * scale
            s_prev = lax.dot_general(qh, kp, dn, preferred_element_type=F32) * scale
            s_own = jnp.where(own_ok, s_own, NEG)
            s_prev = jnp.where(prev_ok, s_prev, NEG)
            m = jnp.maximum(jnp.max(s_own, axis=-1, keepdims=True),
                            jnp.max(s_prev, axis=-1, keepdims=True))
            p_own = jnp.exp(s_own - m)
            p_prev = jnp.exp(s_prev - m)
            l = jnp.sum(p_own, axis=-1, keepdims=True) + jnp.sum(p_prev, axis=-1, keepdims=True)
            acc = jnp.dot(p_own.astype(BF16), vb, preferred_element_type=F32)
            acc = acc + jnp.dot(p_prev.astype(BF16), vp, preferred_element_type=F32)
            o_ref[h, rs, :] = acc / l
            lse_tile = jnp.where(lane == h, m + jnp.log(l), lse_tile)
            kprev_ref[r, h] = kb
            vprev_ref[r, h] = vb
        lse_ref[rs, :] = lse_tile
        return carry

    lax.fori_loop(0, units, unit, 0)


def _attn_prompt(zqkv, gq, gk, gi, tt):
    _, _, n, s, _ = zqkv.shape
    _, dil = ATT_GROUPS[gi]
    spec = lambda c: pl.BlockSpec((None, H_G, None, tt, HD), lambda b, i, c=c: (c, 0, b, i, 0))
    gspec = pl.BlockSpec((1, HD), lambda b, i: (0, 0))
    hspec = pl.BlockSpec((H_G, None, tt, HD), lambda b, i: (0, b, i, 0))
    return pl.pallas_call(
        functools.partial(_attn_prompt_kernel, dil=dil, tt=tt),
        grid=(n, s // tt),
        in_specs=[spec(3 * gi), spec(3 * gi + 1), spec(3 * gi + 2), gspec, gspec],
        out_specs=[hspec, pl.BlockSpec((None, tt, HD), lambda b, i: (b, i, 0)), hspec],
        out_shape=[jax.ShapeDtypeStruct((H_G, n, s, HD), F32),
                   jax.ShapeDtypeStruct((n, s, HD), F32),
                   jax.ShapeDtypeStruct((H_G, n, s, HD), F32)],
        scratch_shapes=[pltpu.VMEM((dil, H_G, ATT_BLK, HD), BF16),
                        pltpu.VMEM((dil, H_G, ATT_BLK, HD), BF16)],
        compiler_params=_cparams(("parallel", "arbitrary")),
        name=f"attn_prompt_g{gi}",
    )(zqkv, zqkv, zqkv, gq, gk)


CC_PAD = 32
CC_RC = 64
CC_LB = 128
CC_NR = 128


def _cconv_prompt_kernel(ga_ref, gb_ref, w_ref, b_ref, ng_ref, nb_ref,
                         c_ref, tail_ref, ext_ref, y_ref, *, tm):
    i = pl.program_id(1)

    nlb = D_CONV // CC_LB

    @pl.when(i == 0)
    def _():
        ext_ref[:, 0:CC_PAD, :] = jnp.zeros((nlb, CC_PAD, CC_LB), F32)

    for lb in range(nlb):
        ls = slice(lb * CC_LB, (lb + 1) * CC_LB)
        ext_ref[lb, CC_PAD:CC_PAD + tm, :] = ga_ref[:, ls] * _sigmoid(gb_ref[:, ls])
        tail_ref[:, ls] = ext_ref[lb, tm:tm + CC_PAD, :]

    off = CC_PAD - (CCONV - 1)
    for lb in range(nlb):
        ls = slice(lb * CC_LB, (lb + 1) * CC_LB)

        def chunk(c, carry, lb=lb, ls=ls):
            r0 = c * CC_RC
            bias = jnp.zeros((8, CC_LB), F32) + b_ref[:, ls]
            acc = [bias] * (CC_RC // 8)
            for k in range(CCONV):
                wk = w_ref[k:k + 1, ls]
                for v in range(CC_RC // 8):
                    acc[v] = acc[v] + wk * ext_ref[lb, pl.ds(r0 + (v + off + k), 8, stride=CC_RC // 8), :]
            for v in range(CC_RC // 8):
                y_ref[lb, pl.ds(r0 + v, 8, stride=CC_RC // 8), :] = acc[v]
            return carry

        lax.fori_loop(0, tm // CC_RC, chunk, 0)
        ext_ref[lb, 0:CC_PAD, :] = ext_ref[lb, tm:tm + CC_PAD, :]

    def norm_chunk(c, carry):
        r0 = pl.multiple_of(c * CC_NR, CC_NR)
        y = jnp.concatenate([y_ref[lb, pl.ds(r0, CC_NR), :] for lb in range(nlb)], axis=-1)
        c_ref[pl.ds(r0, CC_NR), :] = _layernorm_swish(y, ng_ref[...], nb_ref[...]).astype(c_ref.dtype)
        return carry

    lax.fori_loop(0, tm // CC_NR, norm_chunk, 0)


def _cconv_prompt(z, w, b, ng, nb, tm):
    n, s, _ = z.shape
    ca = COL_GLU // D_CONV
    vec = pl.BlockSpec((1, D_CONV), lambda bb, i: (0, 0))
    return pl.pallas_call(
        functools.partial(_cconv_prompt_kernel, tm=tm),
        grid=(n, s // tm),
        in_specs=[pl.BlockSpec((None, tm, D_CONV), lambda bb, i: (bb, i, ca)),
                  pl.BlockSpec((None, tm, D_CONV), lambda bb, i: (bb, i, ca + 1)),
                  pl.BlockSpec((CCONV, D_CONV), lambda bb, i: (0, 0)),
                  vec, vec, vec],
        out_specs=[pl.BlockSpec((None, tm, D_CONV), lambda bb, i: (bb, i, 0)),
                   pl.BlockSpec((None, CC_PAD, D_CONV), lambda bb, i: (bb, 0, 0))],
        out_shape=[jax.ShapeDtypeStruct((n, s, D_CONV), BF16),
                   jax.ShapeDtypeStruct((n, CC_PAD, D_CONV), F32)],
        scratch_shapes=[pltpu.VMEM((D_CONV // CC_LB, tm + CC_PAD, CC_LB), F32),
                        pltpu.VMEM((D_CONV // CC_LB, tm, CC_LB), F32)],
        compiler_params=_cparams(("parallel", "arbitrary")),
        name="cconv_prompt",
    )(z, z, w, b, ng, nb)


MERGE_CB = 512


def _merge_kernel(x_ref, g_ref, h_ref, o0_ref, o1_ref, o2_ref, l0_ref, l1_ref, l2_ref, c_ref,
                  wr_ref, wa_ref, wc_ref, wo_ref, y_ref, att_ref, mrg_ref):
    o_refs = (o0_ref, o1_ref, o2_ref)
    lses = [l_ref[...] for l_ref in (l0_ref, l1_ref, l2_ref)]
    for h in range(H_G):
        lh = [l[:, h:h + 1] for l in lses]
        m = jnp.maximum(jnp.maximum(lh[0], lh[1]), lh[2])
        e = [jnp.exp(v - m) for v in lh]
        den = e[0] + e[1] + e[2]
        hs = slice(h * HD, (h + 1) * HD)
        att = (e[0] / den) * o_refs[0][h]
        att = att + (e[1] / den) * o_refs[1][h]
        att = att + (e[2] / den) * o_refs[2][h]
        att_ref[:, hs] = att.astype(BF16)

    hb = h_ref[...]
    ab = att_ref[...]
    cb = c_ref[...]
    for j in range(D_MODEL // MERGE_CB):
        cs = slice(j * MERGE_CB, (j + 1) * MERGE_CB)
        gs = [slice(br * D_MODEL + j * MERGE_CB, br * D_MODEL + (j + 1) * MERGE_CB)
              for br in range(N_BRANCH)]
        acc = _sigmoid(g_ref[:, gs[0]]) * jnp.dot(hb, wr_ref[:, cs], preferred_element_type=F32)
        acc = acc + _sigmoid(g_ref[:, gs[1]]) * jnp.dot(ab, wa_ref[:, cs], preferred_element_type=F32)
        acc = acc + _sigmoid(g_ref[:, gs[2]]) * jnp.dot(cb, wc_ref[:, cs], preferred_element_type=F32)
        mrg_ref[:, cs] = acc.astype(BF16)

    y_ref[...] = x_ref[...] + jnp.dot(mrg_ref[...], wo_ref[...], preferred_element_type=F32)


def _merge(x2d, z2d, h2d, o_list, lse_list, c2d, wr, wa, wc, wo, tm):
    t, d = x2d.shape
    cg = COL_GATE // (N_BRANCH * D_MODEL)
    row = lambda w: pl.BlockSpec((tm, w), lambda i: (i, 0))
    full = lambda a: pl.BlockSpec(a.shape, lambda i: (0, 0))
    ohead = pl.BlockSpec((H_G, tm, HD), lambda i: (0, i, 0))
    return pl.pallas_call(
        _merge_kernel,
        grid=(t // tm,),
        in_specs=[row(d),
                  pl.BlockSpec((tm, N_BRANCH * D_MODEL), lambda i: (i, cg)),
                  row(D_RNN), ohead, ohead, ohead, row(HD), row(HD), row(HD),
                  row(D_CONV), full(wr), full(wa), full(wc), full(wo)],
        out_specs=row(d),
        out_shape=jax.ShapeDtypeStruct((t, d), F32),
        scratch_shapes=[pltpu.VMEM((tm, ATT_W), BF16), pltpu.VMEM((tm, D_MODEL), BF16)],
        compiler_params=_cparams(("parallel",)),
        name="merge",
    )(x2d, z2d, h2d, *o_list, *lse_list, c2d, wr, wa, wc, wo)


def _gelu(x):
    return 0.5 * x * (1.0 + jnp.tanh(0.7978845608028654 * (x + 0.044715 * (x * x * x))))


def _ffn_conv3(up, hist_ref, j, w_ref, b_ref, ext_ref, tm):
    ext_ref[0:8, :] = hist_ref[j]
    ext_ref[8:8 + tm, :] = up
    y = b_ref[...] + w_ref[2:3, :] * up
    y = y + w_ref[1:2, :] * ext_ref[7:7 + tm, :]
    y = y + w_ref[0:1, :] * ext_ref[6:6 + tm, :]
    hist_ref[j] = ext_ref[tm:tm + 8, :]
    return y


def _ffn_prompt_kernel(x_ref, g_ref, wua_ref, wub_ref, wd_ref, cwa_ref, cwb_ref, cba_ref, cbb_ref,
                       y_ref, ta_ref, tb_ref, xn_ref, acc_ref, ha_ref, hb_ref, ea_ref, eb_ref, *, tm):
    i = pl.program_id(1)
    j = pl.program_id(2)
    nj = pl.num_programs(2)

    @pl.when(j == 0)
    def _():
        x = x_ref[...]
        xn_ref[...] = _rmsnorm_rows(x, g_ref[...]).astype(BF16)
        acc_ref[...] = x

    @pl.when(i == 0)
    def _():
        ha_ref[j] = jnp.zeros(ha_ref.shape[1:], F32)
        hb_ref[j] = jnp.zeros(hb_ref.shape[1:], F32)

    xn = xn_ref[...]
    up_a = jnp.dot(xn, wua_ref[...], preferred_element_type=F32)
    fa = _ffn_conv3(up_a, ha_ref, j, cwa_ref, cba_ref, ea_ref, tm)
    up_b = jnp.dot(xn, wub_ref[...], preferred_element_type=F32)
    fb = _ffn_conv3(up_b, hb_ref, j, cwb_ref, cbb_ref, eb_ref, tm)
    ta_ref[...] = ea_ref[tm:tm + 8, :]
    tb_ref[...] = eb_ref[tm:tm + 8, :]
    gact = (_gelu(fa) * fb).astype(BF16)
    acc_ref[...] += jnp.dot(gact, wd_ref[...], preferred_element_type=F32)

    @pl.when(j == nj - 1)
    def _():
        y_ref[...] = acc_ref[...]


def _ffn_prompt(x, g, wu, wd, cw, cb, tm, hc):
    n, s, d = x.shape
    nj = D_FF // hc
    return pl.pallas_call(
        functools.partial(_ffn_prompt_kernel, tm=tm),
        grid=(n, s // tm, nj),
        in_specs=[pl.BlockSpec((None, tm, d), lambda b, i, j: (b, i, 0)),
                  pl.BlockSpec((1, d), lambda b, i, j: (0, 0)),
                  pl.BlockSpec((d, hc), lambda b, i, j: (0, j)),
                  pl.BlockSpec((d, hc), lambda b, i, j: (0, nj + j)),
                  pl.BlockSpec((hc, d), lambda b, i, j: (j, 0)),
                  pl.BlockSpec((FFN_CONV, hc), lambda b, i, j: (0, j)),
                  pl.BlockSpec((FFN_CONV, hc), lambda b, i, j: (0, nj + j)),
                  pl.BlockSpec((1, hc), lambda b, i, j: (0, j)),
                  pl.BlockSpec((1, hc), lambda b, i, j: (0, nj + j))],
        out_specs=[pl.BlockSpec((None, tm, d), lambda b, i, j: (b, i, 0)),
                   pl.BlockSpec((None, None, 8, hc), lambda b, i, j: (b, i, 0, j)),
                   pl.BlockSpec((None, None, 8, hc), lambda b, i, j: (b, i, 0, j))],
        out_shape=[jax.ShapeDtypeStruct((n, s, d), F32),
                   jax.ShapeDtypeStruct((n, s // tm, 8, D_FF), F32),
                   jax.ShapeDtypeStruct((n, s // tm, 8, D_FF), F32)],
        scratch_shapes=[pltpu.VMEM((tm, d), BF16),
                        pltpu.VMEM((tm, d), F32),
                        pltpu.VMEM((nj, 8, hc), F32),
                        pltpu.VMEM((nj, 8, hc), F32),
                        pltpu.VMEM((tm + 8, hc), F32),
                        pltpu.VMEM((tm + 8, hc), F32)],
        compiler_params=_cparams(("parallel", "arbitrary", "arbitrary")),
        name="ffn_prompt",
    )(x, g, wu, wu, wd, cw, cw, cb, cb)


def _state_sample_kernel(z_ref, rh_ref, rbuf_ref, cbuf_ref,
                         rcw_ref, rcb_ref, wa_ref, wx_ref, ba_ref, bx_ref, lam_ref,
                         ccw_ref, ccb_ref, ng_ref, nb_ref,
                         h_ref, c_ref, rnew_ref, cnew_ref):
    x_rnn = z_ref[:, COL_RNN:COL_RNN + D_RNN]
    u = rcb_ref[...] + rcw_ref[RNN_CONV - 1:RNN_CONV, :] * x_rnn
    for k in range(RNN_CONV - 1):
        row = rbuf_ref[k]
        u = u + rcw_ref[k:k + 1, :] * row
        if k > 0:
            rnew_ref[k - 1] = row
    rnew_ref[RNN_CONV - 2] = x_rnn
    ub = u.astype(BF16)
    r_pre = jnp.concatenate(
        [jnp.dot(ub[:, b * RNN_BW:(b + 1) * RNN_BW], wa_ref[b], preferred_element_type=F32)
         for b in range(RNN_BLOCKS)], axis=-1)
    i_pre = jnp.concatenate(
        [jnp.dot(ub[:, b * RNN_BW:(b + 1) * RNN_BW], wx_ref[b], preferred_element_type=F32)
         for b in range(RNN_BLOCKS)], axis=-1)
    a, uu = _lru_coeffs(u, r_pre, i_pre, ba_ref[...], bx_ref[...], _softplus(-lam_ref[...]))
    h_ref[...] = uu + a * rh_ref[...]

    ga = z_ref[:, COL_GLU:COL_GLU + D_CONV]
    gb = z_ref[:, COL_GLU + D_CONV:COL_GLU + 2 * D_CONV]
    c_in = ga * _sigmoid(gb)
    y = ccb_ref[...] + ccw_ref[CCONV - 1:CCONV, :] * c_in
    for k in range(CCONV - 1):
        row = cbuf_ref[k]
        y = y + ccw_ref[k:k + 1, :] * row
        if k > 0:
            cnew_ref[k - 1] = row
    cnew_ref[CCONV - 2] = c_in
    c_ref[...] = _layernorm_swish(y, ng_ref[...], nb_ref[...]).astype(c_ref.dtype)


def _state_sample(z, rh, rbuf, cbuf, rcw, rcb, wa, wx, ba, bx, lam, ccw, ccb, ng, nbias, nb):
    m = z.shape[0]
    wz = COL_GATE
    row = lambda w: pl.BlockSpec((nb, w), lambda i: (i, 0))
    full = lambda a: pl.BlockSpec(a.shape, lambda i: (0,) * a.ndim)
    rspec = pl.BlockSpec((RNN_CONV - 1, nb, D_RNN), lambda i: (0, i, 0))
    cspec = pl.BlockSpec((CCONV - 1, nb, D_CONV), lambda i: (0, i, 0))
    rbuf = rbuf.transpose(1, 0, 2)
    cbuf = cbuf.transpose(1, 0, 2)
    h, c, rnew, cnew = pl.pallas_call(
        _state_sample_kernel,
        grid=(m // nb,),
        in_specs=[row(wz), row(D_RNN), rspec, cspec,
                  full(rcw), full(rcb), full(wa), full(wx), full(ba), full(bx), full(lam),
                  full(ccw), full(ccb), full(ng), full(nbias)],
        out_specs=[row(D_RNN), row(D_CONV), rspec, cspec],
        out_shape=[jax.ShapeDtypeStruct((m, D_RNN), F32),
                   jax.ShapeDtypeStruct((m, D_CONV), BF16),
                   jax.ShapeDtypeStruct((RNN_CONV - 1, m, D_RNN), F32),
                   jax.ShapeDtypeStruct((CCONV - 1, m, D_CONV), F32)],
        compiler_params=_cparams(("parallel",)),
        name="state_sample",
    )(z, rh, rbuf, cbuf, rcw, rcb, wa, wx, ba, bx, lam, ccw, ccb, ng, nbias)
    return h, c, rnew.transpose(1, 0, 2), cnew.transpose(1, 0, 2)


def _attn_sample_kernel(qkv_ref, kv_ref, gq_ref, gk_ref, o_ref, lse_ref, kvn_ref):
    scale = HD ** -0.5
    qn = _rmsnorm_rows(qkv_ref[:, 0], gq_ref[...])
    kn = _rmsnorm_rows(qkv_ref[:, 1], gk_ref[...])
    vn = qkv_ref[:, 2]
    kvn_ref[:, 0] = kn
    kvn_ref[:, 1] = vn
    kc = kv_ref[:, :, 0]
    vc = kv_ref[:, :, 1]
    s_new = jnp.sum(qn * kn, axis=-1, keepdims=True) * scale
    s_buf = jnp.sum(qn[:, None] * kc, axis=-1, keepdims=True) * scale
    m = jnp.maximum(jnp.max(s_buf, axis=1), s_new)
    p_new = jnp.exp(s_new - m)
    p_buf = jnp.exp(s_buf - m[:, None])
    l = p_new + jnp.sum(p_buf, axis=1)
    acc = p_new * vn + jnp.sum(p_buf * vc, axis=1)
    o_ref[...] = acc / l
    lse_ref[...] = jnp.broadcast_to(m + jnp.log(l), lse_ref.shape)


def _attn_sample(qkv, cache, layer, gq, gk, gi, nb):
    m = qkv.shape[0]
    win, dil = ATT_GROUPS[gi]
    depth, mb, L = cache.shape[:3]
    assert L == win and L // dil == ATT_BLK and mb == m
    view = cache.reshape(depth, m, ATT_BLK, dil, 2, H_G, HD)
    hspec = pl.BlockSpec((nb, H_G, HD), lambda i: (i, 0, 0))
    gspec = pl.BlockSpec((1, 1, HD), lambda i: (0, 0, 0))
    return pl.pallas_call(
        _attn_sample_kernel,
        grid=(m // nb,),
        in_specs=[pl.BlockSpec((nb, None, 3, H_G, HD), lambda i: (i, gi, 0, 0, 0)),
                  pl.BlockSpec((None, nb, ATT_BLK, None, 2, H_G, HD),
                               lambda i: (layer, i, 0, 0, 0, 0, 0)),
                  gspec, gspec],
        out_specs=[hspec, hspec, pl.BlockSpec((nb, 2, H_G, HD), lambda i: (i, 0, 0, 0))],
        out_shape=[jax.ShapeDtypeStruct((m, H_G, HD), F32),
                   jax.ShapeDtypeStruct((m, H_G, HD), F32),
                   jax.ShapeDtypeStruct((m, 2, H_G, HD), F32)],
        compiler_params=_cparams(("parallel",)),
        name=f"attn_sample_g{gi}",
    )(qkv, view, gq.reshape(1, 1, HD), gk.reshape(1, 1, HD))


def _ffn_sample_kernel(x_ref, g_ref, buf_ref, wu_ref, wd_ref, cw_ref, cb_ref,
                       y_ref, new_ref, xn_ref, acc_ref):
    j = pl.program_id(0)
    nj = pl.num_programs(0)

    @pl.when(j == 0)
    def _():
        x = x_ref[...]
        xn_ref[...] = _rmsnorm_rows(x, g_ref[...]).astype(BF16)
        acc_ref[...] = x

    f = []
    for half in range(2):
        up = jnp.dot(xn_ref[...], wu_ref[half], preferred_element_type=F32)
        b0 = buf_ref[0, half]
        b1 = buf_ref[1, half]
        y = cb_ref[half] + cw_ref[half, FFN_CONV - 1:FFN_CONV, :] * up
        y = y + cw_ref[half, 0:1, :] * b0 + cw_ref[half, 1:2, :] * b1
        new_ref[0, half] = b1
        new_ref[1, half] = up
        f.append(y)
    gact = (_gelu(f[0]) * f[1]).astype(BF16)
    acc_ref[...] += jnp.dot(gact, wd_ref[...], preferred_element_type=F32)

    @pl.when(j == nj - 1)
    def _():
        y_ref[...] = acc_ref[...]


def _ffn_sample(x, g, buf, wu, wd, cw, cb, hc):
    m, d = x.shape
    nj = D_FF // hc
    buf4 = buf.reshape(m, FFN_CONV - 1, 2, D_FF).transpose(1, 2, 0, 3)
    wu3 = wu.reshape(d, 2, D_FF).transpose(1, 0, 2)
    cw3 = cw.reshape(FFN_CONV, 2, D_FF).transpose(1, 0, 2)
    cb3 = cb.reshape(2, 1, D_FF)
    bspec = pl.BlockSpec((FFN_CONV - 1, 2, m, hc), lambda j: (0, 0, 0, j))
    y, new = pl.pallas_call(
        _ffn_sample_kernel,
        grid=(nj,),
        in_specs=[pl.BlockSpec((m, d), lambda j: (0, 0)),
                  pl.BlockSpec((1, d), lambda j: (0, 0)),
                  bspec,
                  pl.BlockSpec((2, d, hc), lambda j: (0, 0, j)),
                  pl.BlockSpec((hc, d), lambda j: (j, 0)),
                  pl.BlockSpec((2, FFN_CONV, hc), lambda j: (0, 0, j)),
                  pl.BlockSpec((2, 1, hc), lambda j: (0, 0, j))],
        out_specs=[pl.BlockSpec((m, d), lambda j: (0, 0)), bspec],
        out_shape=[jax.ShapeDtypeStruct((m, d), F32),
                   jax.ShapeDtypeStruct((FFN_CONV - 1, 2, m, D_FF), F32)],
        scratch_shapes=[pltpu.VMEM((m, d), BF16), pltpu.VMEM((m, d), F32)],
        compiler_params=_cparams(("arbitrary",)),
        name="ffn_sample",
    )(x, g, buf4, wu3, wd, cw3, cb3)
    return y, new.transpose(2, 0, 1, 3).reshape(m, FFN_CONV - 1, 2 * D_FF)


def _prep_layer_weights(lw):
    w_in = lw['w_in']
    o1 = D_RNN
    o2 = o1 + N_GROUPS * 3 * ATT_W
    o3 = o2 + 2 * D_CONV
    w_perm = jnp.concatenate([w_in[:, o2:o3], w_in[:, :o1], w_in[:, o3:], w_in[:, o1:o2]], axis=1)
    row = lambda v: v.reshape(1, -1)
    return dict(
        norm1_g=row(lw['norm1_g']), w_in=w_perm.astype(BF16),
        rnn_conv_w=lw['rnn_conv_w'], rnn_conv_b=row(lw['rnn_conv_b']),
        rnn_wa=lw['rnn_wa'].astype(BF16), rnn_wx=lw['rnn_wx'].astype(BF16),
        rnn_ba=row(lw['rnn_ba']), rnn_bx=row(lw['rnn_bx']), rnn_lambda=row(lw['rnn_lambda']),
        q_norm_g=lw['q_norm_g'], k_norm_g=lw['k_norm_g'],
        cconv_w=lw['cconv_w'], cconv_b=row(lw['cconv_b']),
        cnorm_g=row(lw['cnorm_g']), cnorm_b=row(lw['cnorm_b']),
        w_br_rnn=lw['w_br_rnn'].astype(BF16), w_br_attn=lw['w_br_attn'].astype(BF16),
        w_br_conv=lw['w_br_conv'].astype(BF16), w_o=lw['w_o'].astype(BF16),
        norm2_g=row(lw['norm2_g']), ffn_up=lw['ffn_up'].astype(BF16),
        ffn_conv_w=lw['ffn_conv_w'], ffn_conv_b=row(lw['ffn_conv_b']),
        ffn_down=lw['ffn_down'].astype(BF16))


def _prompt_layer(x, w, tiles):
    n, s, d = x.shape
    t = n * s
    z2d, zqkv = _inproj(x.reshape(t, d), w['norm1_g'], w['w_in'], tiles['in_tm'], tiles['in_tn'])
    z = z2d.reshape(n, s, COL_QKV)
    zqkv = zqkv.reshape(N_GROUPS * 3, H_G, n, s, HD)
    h_seq, h_last = _rnn_prompt(z, w['rnn_conv_w'], w['rnn_conv_b'], w['rnn_wa'], w['rnn_wx'],
                                w['rnn_ba'], w['rnn_bx'], w['rnn_lambda'], tiles['rnn_tm'])
    o_list, lse_list, kv_new = [], [], []
    for gi, (win, dil) in enumerate(ATT_GROUPS):
        o, lse, kn = _attn_prompt(zqkv, w['q_norm_g'][gi:gi + 1], w['k_norm_g'][gi:gi + 1], gi,
                                  tiles['att_tt'])
        o_list.append(o.reshape(H_G, t, HD))
        lse_list.append(lse.reshape(t, HD))
        keep = min(win, s)
        k_tail = kn[:, :, s - keep:].transpose(1, 2, 0, 3)
        v_tail = zqkv[3 * gi + 2, :, :, s - keep:].transpose(1, 2, 0, 3)
        kv_new.append(jnp.stack([k_tail, v_tail], axis=2))
    c, c_tail = _cconv_prompt(z, w['cconv_w'], w['cconv_b'], w['cnorm_g'], w['cnorm_b'],
                              tiles['cc_tm'])
    x1 = _merge(x.reshape(t, d), z2d, h_seq.reshape(t, D_RNN), o_list, lse_list,
                c.reshape(t, D_CONV), w['w_br_rnn'], w['w_br_attn'], w['w_br_conv'], w['w_o'],
                tiles['mrg_tm'])
    x2, ta, tb = _ffn_prompt(x1.reshape(n, s, d), w['norm2_g'], w['ffn_up'], w['ffn_down'],
                             w['ffn_conv_w'], w['ffn_conv_b'], tiles['ffn_tm'], tiles['ffn_hc'])
    rnn_buf_new = z[:, s - (RNN_CONV - 1):, COL_RNN:COL_RNN + D_RNN]
    cconv_new = c_tail[:, CC_PAD - (CCONV - 1):]
    ffn_new = jnp.concatenate([ta[:, -1, 8 - (FFN_CONV - 1):], tb[:, -1, 8 - (FFN_CONV - 1):]],
                              axis=-1)
    return x2, kv_new, h_last.reshape(n, D_RNN), rnn_buf_new, cconv_new, ffn_new


def _sample_layer(x, w, layer, rnn_h, rnn_buf, cconv_buf, ffn_buf, caches, tiles):
    m, _, d = x.shape
    x2d = x.reshape(m, d)
    z, zqkv = _inproj(x2d, w['norm1_g'], w['w_in'], m, tiles['in_tn'])
    h, c, rnew, cnew = _state_sample(
        z, rnn_h, rnn_buf, cconv_buf, w['rnn_conv_w'], w['rnn_conv_b'], w['rnn_wa'], w['rnn_wx'],
        w['rnn_ba'], w['rnn_bx'], w['rnn_lambda'], w['cconv_w'], w['cconv_b'],
        w['cnorm_g'], w['cnorm_b'], tiles['smp_state_nb'])
    qkv = zqkv.transpose(1, 0, 2).reshape(m, N_GROUPS, 3, H_G, HD)
    o_list, lse_list, kv_new = [], [], []
    for gi in range(N_GROUPS):
        o, lse, kvn = _attn_sample(qkv, caches[gi], layer, w['q_norm_g'][gi], w['k_norm_g'][gi],
                                   gi, tiles['smp_attn_nb'])
        o_list.append(o.transpose(1, 0, 2))
        lse_list.append(jnp.pad(lse[:, :, 0], ((0, 0), (0, HD - H_G))))
        kv_new.append(kvn.reshape(m, 1, 2, H_G, HD))
    x1 = _merge(x2d, z, h.astype(BF16), o_list, lse_list, c,
                w['w_br_rnn'], w['w_br_attn'], w['w_br_conv'], w['w_o'], m)
    x2, ffn_new = _ffn_sample(x1, w['norm2_g'], ffn_buf, w['ffn_up'], w['ffn_down'],
                              w['ffn_conv_w'], w['ffn_conv_b'], tiles['ffn_hc'])
    return x2.reshape(m, 1, d), kv_new, h, rnew, cnew, ffn_new


def _tiles(s):
    return dict(in_tm=min(1024, s), in_tn=1536, rnn_tm=min(1024, s), att_tt=2048,
                cc_tm=min(1024, s), mrg_tm=min(512, s), ffn_tm=min(1024, s), ffn_hc=1024,
                smp_state_nb=32, smp_attn_nb=8)


def kernel(x_prompt, x_sample, cache_kv_w128, cache_kv_w512, cache_kv_w2048, state_rnn_h, state_rnn_conv, state_cconv, state_ffn_conv, norm1_g, w_in, rnn_conv_w, rnn_conv_b, rnn_wa, rnn_ba, rnn_wx, rnn_bx, rnn_lambda, q_norm_g, k_norm_g, cconv_w, cconv_b, cnorm_g, cnorm_b, w_br_rnn, w_br_attn, w_br_conv, w_o, norm2_g, ffn_up, ffn_conv_w, ffn_conv_b, ffn_down):
    params = dict(norm1_g=norm1_g, w_in=w_in, rnn_conv_w=rnn_conv_w, rnn_conv_b=rnn_conv_b,
                  rnn_wa=rnn_wa, rnn_ba=rnn_ba, rnn_wx=rnn_wx, rnn_bx=rnn_bx,
                  rnn_lambda=rnn_lambda, q_norm_g=q_norm_g, k_norm_g=k_norm_g, cconv_w=cconv_w,
                  cconv_b=cconv_b, cnorm_g=cnorm_g, cnorm_b=cnorm_b, w_br_rnn=w_br_rnn,
                  w_br_attn=w_br_attn, w_br_conv=w_br_conv, w_o=w_o, norm2_g=norm2_g,
                  ffn_up=ffn_up, ffn_conv_w=ffn_conv_w, ffn_conv_b=ffn_conv_b, ffn_down=ffn_down)
    depth = w_in.shape[0]
    assert x_sample.shape[1] == 1
    caches = (cache_kv_w128, cache_kv_w512, cache_kv_w2048)
    tiles = _tiles(x_prompt.shape[1])
    xp, xs = x_prompt, x_sample
    pk, sk = [[], [], []], [[], [], []]
    p_state = [[], [], [], []]
    s_state = [[], [], [], []]
    for l in range(depth):
        w = _prep_layer_weights({k: v[l] for k, v in params.items()})
        xp, kvp, hp, rbp, cbp, fbp = _prompt_layer(xp, w, tiles)
        xs, kvs, hs, rbs, cbs, fbs = _sample_layer(
            xs, w, l, state_rnn_h[l], state_rnn_conv[l], state_cconv[l], state_ffn_conv[l],
            caches, tiles)
        for gi in range(N_GROUPS):
            pk[gi].append(kvp[gi])
            sk[gi].append(kvs[gi])
        for lst, v in zip(p_state, (hp, rbp, cbp, fbp)):
            lst.append(v)
        for lst, v in zip(s_state, (hs, rbs, cbs, fbs)):
            lst.append(v)
    stack = lambda xs_: jnp.stack(xs_, 0)
    return (xp, xs, stack(pk[0]), stack(pk[1]), stack(pk[2]),
            stack(p_state[0]), stack(p_state[1]), stack(p_state[2]), stack(p_state[3]),
            stack(sk[0]), stack(sk[1]), stack(sk[2]),
            stack(s_state[0]), stack(s_state[1]), stack(s_state[2]), stack(s_state[3]))
```

```python
import functools

import jax
import jax.numpy as jnp
from jax import lax
from jax.experimental import pallas as pl
from jax.experimental.pallas import tpu as pltpu

F32 = jnp.float32
BF16 = jnp.bfloat16

D_MODEL = 1024
D_RNN = D_MODEL
RNN_BLOCKS = 8
RNN_BW = D_RNN // RNN_BLOCKS
RNN_CONV = 4
LRU_C = 8.0
ATT_GROUPS = ((128, 1), (512, 4), (2048, 16))
N_GROUPS = len(ATT_GROUPS)
H_G = 4
HD = 128
ATT_W = H_G * HD
ATT_BLK = 128
D_CONV = D_MODEL
CCONV = 31
D_FF = 3 * D_MODEL
FFN_CONV = 3
N_BRANCH = 3
EPS = 1e-6
NEG = -1e30

COL_GLU = 0
COL_RNN = 2 * D_CONV
COL_GATE = COL_RNN + D_RNN
COL_QKV = COL_GATE + N_BRANCH * D_MODEL
N_IN = COL_QKV + N_GROUPS * 3 * ATT_W

VMEM_LIMIT = 56 * 1024 * 1024


def _cparams(sem):
    return pltpu.CompilerParams(dimension_semantics=sem, vmem_limit_bytes=VMEM_LIMIT)


def _sigmoid(x):
    return 0.5 * jnp.tanh(0.5 * x) + 0.5


def _softplus(x):
    return jnp.maximum(x, 0.0) + jnp.log1p(jnp.exp(-jnp.abs(x)))


def _rmsnorm_rows(x, g):
    ms = jnp.mean(x * x, axis=-1, keepdims=True)
    return x * lax.rsqrt(ms + EPS) * g


def _layernorm_swish(y, g, b):
    mu = jnp.mean(y, axis=-1, keepdims=True)
    yc = y - mu
    var = jnp.mean(yc * yc, axis=-1, keepdims=True)
    yn = yc * lax.rsqrt(var + EPS) * g + b
    return yn * _sigmoid(yn)


def _inproj_kernel(x_ref, g_ref, w_ref, o_ref, qkv_ref, xn_ref, *, n_main):
    j = pl.program_id(1)

    @pl.when(j == 0)
    def _():
        xn_ref[...] = _rmsnorm_rows(x_ref[...], g_ref[...]).astype(BF16)

    res = jnp.dot(xn_ref[...], w_ref[...], preferred_element_type=F32)

    @pl.when(j < n_main)
    def _():
        o_ref[...] = res.astype(o_ref.dtype)

    @pl.when(j >= n_main)
    def _():
        for c in range(qkv_ref.shape[0]):
            qkv_ref[c] = res[:, c * HD:(c + 1) * HD]


def _inproj(x2d, g, w, tm, tn):
    t, d = x2d.shape
    n_main = COL_QKV // tn
    n_qkv = (N_IN - COL_QKV) // tn
    hpt = tn // HD
    return pl.pallas_call(
        functools.partial(_inproj_kernel, n_main=n_main),
        grid=(t // tm, n_main + n_qkv),
        in_specs=[pl.BlockSpec((tm, d), lambda i, j: (i, 0)),
                  pl.BlockSpec((1, d), lambda i, j: (0, 0)),
                  pl.BlockSpec((d, tn), lambda i, j: (0, j))],
        out_specs=[pl.BlockSpec((tm, tn), lambda i, j: (i, jnp.minimum(j, n_main - 1))),
                   pl.BlockSpec((hpt, tm, HD), lambda i, j: (jnp.maximum(j - n_main, 0), i, 0))],
        out_shape=[jax.ShapeDtypeStruct((t, COL_QKV), BF16),
                   jax.ShapeDtypeStruct((n_qkv * hpt, t, HD), F32)],
        scratch_shapes=[pltpu.VMEM((tm, d), BF16)],
        compiler_params=_cparams(("parallel", "arbitrary")),
        name="inproj",
    )(x2d, g, w)


def _lru_coeffs(u, r_pre, i_pre, ba, bx, sp):
    r = _sigmoid(r_pre + ba)
    ig = _sigmoid(i_pre + bx)
    log_a = (-LRU_C * r) * sp
    a = jnp.exp(log_a)
    mult = jnp.sqrt(-jnp.tanh(log_a) * (a * a + 1.0))
    return a, mult * (ig * u)


RNN_RC = 256


def _rnn_prompt_kernel(x_ref, cw_ref, cb_ref, wa_ref, wx_ref, ba_ref, bx_ref, lam_ref,
                       h_ref, hlast_ref, ext_ref, hp_ref, hcar_ref, *, tm):
    i = pl.program_id(1)
    nch = RNN_RC // 64
    rpv = RNN_RC // 8

    @pl.when(i == 0)
    def _():
        ext_ref[:, 0:8, :] = jnp.zeros((RNN_BLOCKS, 8, RNN_BW), F32)
        hcar_ref[...] = jnp.zeros((1, D_RNN), F32)

    for b in range(RNN_BLOCKS):
        ext_ref[b, 8:8 + tm, :] = x_ref[:, b * RNN_BW:(b + 1) * RNN_BW].astype(F32)

    row = lax.broadcasted_iota(jnp.int32, (8, RNN_BW), 0)

    for b in range(RNN_BLOCKS):
        cs = slice(b * RNN_BW, (b + 1) * RNN_BW)
        sp = _softplus(-lam_ref[:, cs])
        cb = cb_ref[:, cs]
        ba = ba_ref[:, cs]
        bx = bx_ref[:, cs]
        cw = [cw_ref[k:k + 1, cs] for k in range(RNN_CONV)]
        wa = wa_ref[b]
        wx = wx_ref[b]

        def gate_chunk(c, car, b=b, cs=cs, sp=sp, cb=cb, ba=ba, bx=bx, cw=cw, wa=wa, wx=wx):
            r0 = pl.multiple_of(c * RNN_RC, RNN_RC)
            us = []
            for v in range(8):
                u = cb
                for k in range(RNN_CONV):
                    lo = r0 + (v + 8 - (RNN_CONV - 1) + k)
                    u = u + cw[k] * ext_ref[b, pl.ds(lo, rpv, stride=8), :]
                us.append(u)
            u_all = jnp.concatenate(us, axis=0)
            ub = u_all.astype(BF16)
            r_pre = jnp.dot(ub, wa, preferred_element_type=F32)
            i_pre = jnp.dot(ub, wx, preferred_element_type=F32)
            a_all, uu_all = _lru_coeffs(u_all, r_pre, i_pre, ba, bx, sp)
            hs, ps = [], []
            for v in range(8):
                a = a_all[v * rpv:(v + 1) * rpv]
                uu = uu_all[v * rpv:(v + 1) * rpv]
                h = uu if v == 0 else a * h + uu
                p = a if v == 0 else a * p
                hs.append(h)
                ps.append(p)
            cins = []
            for q in range(nch):
                pe, he = _sublane_scan(p[8 * q:8 * q + 8], h[8 * q:8 * q + 8], row)
                e = he + pe * car
                cins.append(jnp.where(row == 0, car, pltpu.roll(e, 1, axis=0)))
                car = e[7:8, :]
            cin = jnp.concatenate(cins, axis=0)
            for v in range(8):
                hp_ref[b, v * rpv:(v + 1) * rpv, :] = hs[v] + ps[v] * cin
            for j in range(0, rpv, 2):
                t0 = hp_ref[b, pl.ds(j, 8, stride=rpv), :]
                t1 = hp_ref[b, pl.ds(j + 1, 8, stride=rpv), :]
                dst = pl.multiple_of(r0 + 8 * j, 16)
                h_ref[pl.ds(dst, 16), cs] = jnp.concatenate([t0, t1], axis=0).astype(h_ref.dtype)
            return car

        car = lax.fori_loop(0, tm // RNN_RC, gate_chunk, hcar_ref[:, cs])
        hcar_ref[:, cs] = car
        hlast_ref[:, cs] = car
        ext_ref[b, 0:8, :] = ext_ref[b, tm:tm + 8, :]


def _sublane_scan(p, h, row):
    for d in (1, 2, 4):
        hr = pltpu.roll(h, d, axis=0)
        pr = pltpu.roll(p, d, axis=0)
        ok = row >= d
        h = jnp.where(ok, h + p * hr, h)
        p = jnp.where(ok, p * pr, p)
    return p, h


def _rnn_prompt(z, cw, cb, wa, wx, ba, bx, lam, tm):
    n, s, _ = z.shape
    assert tm % RNN_RC == 0
    cblk = COL_RNN // D_RNN
    vec = pl.BlockSpec((1, D_RNN), lambda b, i: (0, 0))
    blkw = pl.BlockSpec((RNN_BLOCKS, RNN_BW, RNN_BW), lambda b, i: (0, 0, 0))
    return pl.pallas_call(
        functools.partial(_rnn_prompt_kernel, tm=tm),
        grid=(n, s // tm),
        in_specs=[pl.BlockSpec((None, tm, D_RNN), lambda b, i: (b, i, cblk)),
                  pl.BlockSpec((RNN_CONV, D_RNN), lambda b, i: (0, 0)),
                  vec, blkw, blkw, vec, vec, vec],
        out_specs=[pl.BlockSpec((None, tm, D_RNN), lambda b, i: (b, i, 0)),
                   pl.BlockSpec((None, 1, D_RNN), lambda b, i: (b, 0, 0))],
        out_shape=[jax.ShapeDtypeStruct((n, s, D_RNN), BF16),
                   jax.ShapeDtypeStruct((n, 1, D_RNN), F32)],
        scratch_shapes=[pltpu.VMEM((RNN_BLOCKS, tm + 8, RNN_BW), F32),
                        pltpu.VMEM((RNN_BLOCKS, RNN_RC, RNN_BW), F32),
                        pltpu.VMEM((1, D_RNN), F32)],
        compiler_params=_cparams(("parallel", "arbitrary")),
        name="rnn_prompt",
    )(z, cw, cb, wa, wx, ba, bx, lam)


ATT_UNROLL = 2


def _attn_prompt_kernel(q_ref, k_ref, v_ref, gq_ref, gk_ref,
                        o_ref, lse_ref, kn_ref, ks_ref, vs_ref, knt_ref, *, dil, tt, nblk_total):
    ti = pl.program_id(1)
    nblk = tt // (ATT_BLK * dil)
    units = tt // ATT_BLK
    sb = tt // nblk

    @pl.when(ti == 0)
    def _():
        ks_ref[...] = jnp.zeros(ks_ref.shape, BF16)
        vs_ref[...] = jnp.zeros(vs_ref.shape, BF16)

    sa = lax.broadcasted_iota(jnp.int32, (ATT_BLK, ATT_BLK), 0)
    sk = lax.broadcasted_iota(jnp.int32, (ATT_BLK, ATT_BLK), 1)
    if dil == 4:
        half = ATT_BLK // 2
        pos = lambda a: 2 * (a & (half - 1)) + (a >> (half.bit_length() - 1))
    else:
        pos = lambda a: a
    own_ok = pos(sk) <= pos(sa)
    prev_band = pos(sk) >= pos(sa)
    lane = lax.broadcasted_iota(jnp.int32, (ATT_BLK, HD), 1)
    ones = jnp.ones((ATT_BLK, HD), BF16)
    gq = gq_ref[...] * (HD ** -0.5)
    gk = gk_ref[...]
    dn = (((1,), (1,)), ((), ()))
    upb = units // nblk
    sh = upb.bit_length() - 1

    def load(ref, h, jb, r):
        base = jb * sb + r
        if dil == 1:
            return ref[h, pl.ds(pl.multiple_of(base, ATT_BLK), ATT_BLK), :]
        if dil == 4:
            return jnp.concatenate([ref[h, pl.ds(base, ATT_BLK // 2, stride=8), :],
                                    ref[h, pl.ds(base + 4, ATT_BLK // 2, stride=8), :]], axis=0)
        return ref[h, pl.ds(base, ATT_BLK, stride=dil), :]

    def store_natural(ref, h, r, val):
        if dil == 1:
            ref[h] = val
        elif dil == 4:
            ref[h, pl.ds(r, ATT_BLK // 2, stride=8), :] = val[:ATT_BLK // 2]
            ref[h, pl.ds(r + 4, ATT_BLK // 2, stride=8), :] = val[ATT_BLK // 2:]
        else:
            ref[h, pl.ds(r, ATT_BLK, stride=dil), :] = val

    def unit_group(g, carry):
        us = [g * ATT_UNROLL + e for e in range(ATT_UNROLL)]
        jb = [u >> sh for u in us]
        r = [u & (upb - 1) for u in us]
        gblk = [ti * nblk + j for j in jb]
        par = [gb & 1 for gb in gblk]
        prev_ok = [jnp.logical_and(prev_band, gb > 0) for gb in gblk]
        dst = [_unit_rows(dil, jb[e], sb, r[e]) for e in range(ATT_UNROLL)]
        eh = [(e, h) for e in range(ATT_UNROLL) for h in range(H_G)]
        qb = {k: _rmsnorm_rows(load(q_ref, k[1], jb[k[0]], r[k[0]]), gq).astype(BF16) for k in eh}
        kn = {k: _rmsnorm_rows(load(k_ref, k[1], jb[k[0]], r[k[0]]), gk) for k in eh}
        for e, h in eh:
            knt_ref[e, h] = kn[e, h]
        kb = {k: kn[k].astype(BF16) for k in eh}
        vb = {k: jnp.concatenate([load(v_ref, k[1], jb[k[0]], r[k[0]]).astype(BF16), ones], axis=1)
              for k in eh}
        kprev, vprev = {}, {}
        for e, h in eh:
            if upb == 1 and e > 0:
                kprev[e, h], vprev[e, h] = kb[e - 1, h], vb[e - 1, h]
            else:
                kprev[e, h] = ks_ref[r[e], h, 1 - par[e]]
                vprev[e, h] = vs_ref[r[e], h, 1 - par[e]]
        for e, h in eh:
            ks_ref[r[e], h, par[e]] = kb[e, h]
            vs_ref[r[e], h, par[e]] = vb[e, h]
        s = {k: lax.dot_general(qb[k], jnp.concatenate([kprev[k], kb[k]], axis=0), dn,
                                preferred_element_type=F32) for k in eh}
        s_prev = {k: jnp.where(prev_ok[k[0]], s[k][:, :ATT_BLK], NEG) for k in eh}
        s_own = {k: jnp.where(own_ok, s[k][:, ATT_BLK:], NEG) for k in eh}
        m = {k: jnp.max(jnp.maximum(s_prev[k], s_own[k]), axis=-1, keepdims=True) for k in eh}
        p = {k: jnp.concatenate([jnp.exp(s_prev[k] - m[k]), jnp.exp(s_own[k] - m[k])],
                                axis=1).astype(BF16) for k in eh}
        acc = {k: jnp.dot(p[k], jnp.concatenate([vprev[k], vb[k]], axis=0),
                          preferred_element_type=F32) for k in eh}
        for e in range(ATT_UNROLL):
            lse_tile = jnp.zeros((ATT_BLK, HD), F32)
            for h in range(H_G):
                l = acc[e, h][:, HD:]
                o = acc[e, h][:, :HD] / l
                for lo, n_rows, src in dst[e]:
                    o_ref[h, pl.ds(lo, n_rows), :] = o[src:src + n_rows]
                lse_tile = jnp.where(lane == h, m[e, h] + jnp.log(l), lse_tile)
            for lo, n_rows, src in dst[e]:
                lse_ref[pl.ds(lo, n_rows), :] = lse_tile[src:src + n_rows]

        for e in range(ATT_UNROLL):
            @pl.when(gblk[e] == nblk_total - 1)
            def _(e=e):
                for h in range(H_G):
                    store_natural(kn_ref, h, r[e], knt_ref[e, h])

        return carry

    lax.fori_loop(0, units // ATT_UNROLL, unit_group, 0)


def _unit_rows(dil, jb, sb, r):
    base = jb * sb
    if dil == 1:
        return [(pl.multiple_of(base, ATT_BLK), ATT_BLK, 0)]
    if dil == 4:
        half = ATT_BLK // 2
        return [(pl.multiple_of(base + half * r, half), half, 0),
                (pl.multiple_of(base + sb // 2 + half * r, half), half, half)]
    return [(pl.multiple_of(base + ATT_BLK * r, ATT_BLK), ATT_BLK, 0)]


def _attn_prompt(zqkv, gq, gk, gi, tt):
    _, _, n, s, _ = zqkv.shape
    win, dil = ATT_GROUPS[gi]
    assert win == ATT_BLK * dil and tt % win == 0 and s % tt == 0
    upb = tt // ATT_BLK // (tt // win)
    spec = lambda c: pl.BlockSpec((None, H_G, None, tt, HD), lambda b, i, c=c: (c, 0, b, i, 0))
    gspec = pl.BlockSpec((1, HD), lambda b, i: (0, 0))
    hspec = pl.BlockSpec((H_G, None, tt, HD), lambda b, i: (0, b, i, 0))
    return pl.pallas_call(
        functools.partial(_attn_prompt_kernel, dil=dil, tt=tt, nblk_total=s // win),
        grid=(n, s // tt),
        in_specs=[spec(3 * gi), spec(3 * gi + 1), spec(3 * gi + 2), gspec, gspec],
        out_specs=[hspec, pl.BlockSpec((None, tt, HD), lambda b, i: (b, i, 0)),
                   pl.BlockSpec((H_G, None, win, HD), lambda b, i: (0, b, 0, 0))],
        out_shape=[jax.ShapeDtypeStruct((H_G, n, s, HD), F32),
                   jax.ShapeDtypeStruct((n, s, HD), F32),
                   jax.ShapeDtypeStruct((H_G, n, win, HD), F32)],
        scratch_shapes=[pltpu.VMEM((upb, H_G, 2, ATT_BLK, HD), BF16),
                        pltpu.VMEM((upb, H_G, 2, ATT_BLK, 2 * HD), BF16),
                        pltpu.VMEM((ATT_UNROLL, H_G, ATT_BLK, HD), F32)],
        compiler_params=_cparams(("parallel", "arbitrary")),
        name=f"attn_prompt_g{gi}",
    )(zqkv, zqkv, zqkv, gq, gk)


CC_PAD = 32
CC_RC = 64
CC_LB = 128


def _cconv_prompt_kernel(ga_ref, gb_ref, w_ref, b_ref, ng_ref, nb_ref,
                         c_ref, tail_ref, ext_ref, y_ref, *, tm):
    i = pl.program_id(1)

    nlb = D_CONV // CC_LB

    @pl.when(i == 0)
    def _():
        ext_ref[:, 0:CC_PAD, :] = jnp.zeros((nlb, CC_PAD, CC_LB), F32)

    for lb in range(nlb):
        ls = slice(lb * CC_LB, (lb + 1) * CC_LB)
        ext_ref[lb, CC_PAD:CC_PAD + tm, :] = (ga_ref[:, ls].astype(F32)
                                              * _sigmoid(gb_ref[:, ls].astype(F32)))
        tail_ref[:, ls] = ext_ref[lb, tm:tm + CC_PAD, :]

    off = CC_PAD - (CCONV - 1)
    for lb in range(nlb):
        ls = slice(lb * CC_LB, (lb + 1) * CC_LB)

        def chunk(c, carry, lb=lb, ls=ls):
            r0 = c * CC_RC
            bias = jnp.zeros((8, CC_LB), F32) + b_ref[:, ls]
            acc = [bias] * (CC_RC // 8)
            for k in range(CCONV):
                wk = w_ref[k:k + 1, ls]
                for v in range(CC_RC // 8):
                    acc[v] = acc[v] + wk * ext_ref[lb, pl.ds(r0 + (v + off + k), 8, stride=CC_RC // 8), :]
            y_ref[lb, pl.ds(pl.multiple_of(r0, CC_RC), CC_RC), :] = jnp.concatenate(acc, axis=0)
            return carry

        lax.fori_loop(0, tm // CC_RC, chunk, 0)
        ext_ref[lb, 0:CC_PAD, :] = ext_ref[lb, tm:tm + CC_PAD, :]

    def norm_chunk(c, carry):
        r0 = pl.multiple_of(c * CC_RC, CC_RC)
        y = jnp.concatenate(
            [jnp.concatenate([y_ref[lb, pl.ds(r0 + j, 8, stride=CC_RC // 8), :]
                              for j in range(CC_RC // 8)], axis=0)
             for lb in range(nlb)], axis=-1)
        c_ref[pl.ds(r0, CC_RC), :] = _layernorm_swish(y, ng_ref[...], nb_ref[...]).astype(c_ref.dtype)
        return carry

    lax.fori_loop(0, tm // CC_RC, norm_chunk, 0)


def _cconv_prompt(z, w, b, ng, nb, tm):
    n, s, _ = z.shape
    ca = COL_GLU // D_CONV
    vec = pl.BlockSpec((1, D_CONV), lambda bb, i: (0, 0))
    return pl.pallas_call(
        functools.partial(_cconv_prompt_kernel, tm=tm),
        grid=(n, s // tm),
        in_specs=[pl.BlockSpec((None, tm, D_CONV), lambda bb, i: (bb, i, ca)),
                  pl.BlockSpec((None, tm, D_CONV), lambda bb, i: (bb, i, ca + 1)),
                  pl.BlockSpec((CCONV, D_CONV), lambda bb, i: (0, 0)),
                  vec, vec, vec],
        out_specs=[pl.BlockSpec((None, tm, D_CONV), lambda bb, i: (bb, i, 0)),
                   pl.BlockSpec((None, CC_PAD, D_CONV), lambda bb, i: (bb, 0, 0))],
        out_shape=[jax.ShapeDtypeStruct((n, s, D_CONV), BF16),
                   jax.ShapeDtypeStruct((n, CC_PAD, D_CONV), F32)],
        scratch_shapes=[pltpu.VMEM((D_CONV // CC_LB, tm + CC_PAD, CC_LB), F32),
                        pltpu.VMEM((D_CONV // CC_LB, tm, CC_LB), F32)],
        compiler_params=_cparams(("parallel", "arbitrary")),
        name="cconv_prompt",
    )(z, z, w, b, ng, nb)


MERGE_CB = 512


def _merge_kernel(x_ref, g_ref, h_ref, o0_ref, o1_ref, o2_ref, l0_ref, l1_ref, l2_ref, c_ref,
                  wr_ref, wa_ref, wc_ref, wo_ref, y_ref, att_ref, mrg_ref, *, unit_major, tm):
    o_refs = (o0_ref, o1_ref, o2_ref)
    l_refs = (l0_ref, l1_ref, l2_ref)
    i = pl.program_id(0)

    def rows(ref, lead, g):
        pre = () if lead is None else (lead,)
        dil = ATT_GROUPS[g][1]
        if not unit_major or dil == 1:
            return ref[pre + (slice(None), slice(None))]
        nv = tm // 8
        if dil == 4:
            starts = [(j, tm // 8) for j in range(nv)]
        else:
            sb = ATT_GROUPS[g][0]
            q = (i % (sb // tm)) * (tm // 16)
            starts = [(q + ((sb // 2) * (j % 2) + j // 2), ATT_BLK) for j in range(nv)]
        return jnp.concatenate(
            [ref[pre + (pl.ds(st, 8, stride=sd), slice(None))] for st, sd in starts], axis=0)

    lses = [rows(l_refs[g], None, g) for g in range(N_GROUPS)]
    for h in range(H_G):
        lh = [l[:, h:h + 1] for l in lses]
        m = jnp.maximum(jnp.maximum(lh[0], lh[1]), lh[2])
        e = [jnp.exp(v - m) for v in lh]
        den = e[0] + e[1] + e[2]
        hs = slice(h * HD, (h + 1) * HD)
        att = (e[0] / den) * rows(o_refs[0], h, 0)
        att = att + (e[1] / den) * rows(o_refs[1], h, 1)
        att = att + (e[2] / den) * rows(o_refs[2], h, 2)
        att_ref[:, hs] = att.astype(BF16)

    hb = h_ref[...]
    ab = att_ref[...]
    cb = c_ref[...]
    for j in range(D_MODEL // MERGE_CB):
        cs = slice(j * MERGE_CB, (j + 1) * MERGE_CB)
        gs = [slice(br * D_MODEL + j * MERGE_CB, br * D_MODEL + (j + 1) * MERGE_CB)
              for br in range(N_BRANCH)]
        gate = [_sigmoid(g_ref[:, sl].astype(F32)) for sl in gs]
        acc = gate[0] * jnp.dot(hb, wr_ref[:, cs], preferred_element_type=F32)
        acc = acc + gate[1] * jnp.dot(ab, wa_ref[:, cs], preferred_element_type=F32)
        acc = acc + gate[2] * jnp.dot(cb, wc_ref[:, cs], preferred_element_type=F32)
        mrg_ref[:, cs] = acc.astype(BF16)

    y_ref[...] = x_ref[...] + jnp.dot(mrg_ref[...], wo_ref[...], preferred_element_type=F32)


def _merge(x2d, z2d, h2d, o_list, lse_list, c2d, wr, wa, wc, wo, tm, unit_major):
    t, d = x2d.shape
    cg = COL_GATE // (N_BRANCH * D_MODEL)
    row = lambda w: pl.BlockSpec((tm, w), lambda i: (i, 0))
    full = lambda a: pl.BlockSpec(a.shape, lambda i: (0, 0))
    ohead = pl.BlockSpec((H_G, tm, HD), lambda i: (0, i, 0))
    o_specs, l_specs = [], []
    for win, dil in ATT_GROUPS:
        if unit_major and dil > 1:
            assert win % tm == 0 and (dil != 4 or win == tm) and tm % 16 == 0
            per = win // tm
            o_specs.append(pl.BlockSpec((H_G, win, HD), lambda i, per=per: (0, i // per, 0)))
            l_specs.append(pl.BlockSpec((win, HD), lambda i, per=per: (i // per, 0)))
        else:
            o_specs.append(ohead)
            l_specs.append(row(HD))
    return pl.pallas_call(
        functools.partial(_merge_kernel, unit_major=unit_major, tm=tm),
        grid=(t // tm,),
        in_specs=[row(d),
                  pl.BlockSpec((tm, N_BRANCH * D_MODEL), lambda i: (i, cg)),
                  row(D_RNN), *o_specs, *l_specs,
                  row(D_CONV), full(wr), full(wa), full(wc), full(wo)],
        out_specs=row(d),
        out_shape=jax.ShapeDtypeStruct((t, d), F32),
        scratch_shapes=[pltpu.VMEM((tm, ATT_W), BF16), pltpu.VMEM((tm, D_MODEL), BF16)],
        compiler_params=_cparams(("parallel",)),
        name="merge",
    )(x2d, z2d, h2d, *o_list, *lse_list, c2d, wr, wa, wc, wo)


def _gelu(x):
    return 0.5 * x * (1.0 + jnp.tanh(0.7978845608028654 * (x + 0.044715 * (x * x * x))))


def _ffn_conv3(up, hist_ref, j, w_ref, b_ref, ext_ref, tm):
    ext_ref[0:8, :] = hist_ref[j]
    ext_ref[8:8 + tm, :] = up
    y = b_ref[...] + w_ref[2:3, :] * up
    y = y + w_ref[1:2, :] * ext_ref[7:7 + tm, :]
    y = y + w_ref[0:1, :] * ext_ref[6:6 + tm, :]
    hist_ref[j] = ext_ref[tm:tm + 8, :]
    return y


def _ffn_prompt_kernel(x_ref, g_ref, wua_ref, wub_ref, wd_ref, cwa_ref, cwb_ref, cba_ref, cbb_ref,
                       y_ref, ta_ref, tb_ref, xn_ref, acc_ref, ha_ref, hb_ref, ea_ref, eb_ref, *, tm):
    i = pl.program_id(1)
    j = pl.program_id(2)
    nj = pl.num_programs(2)

    @pl.when(j == 0)
    def _():
        x = x_ref[...]
        xn_ref[...] = _rmsnorm_rows(x, g_ref[...]).astype(BF16)
        acc_ref[...] = x

    @pl.when(i == 0)
    def _():
        ha_ref[j] = jnp.zeros(ha_ref.shape[1:], F32)
        hb_ref[j] = jnp.zeros(hb_ref.shape[1:], F32)

    xn = xn_ref[...]
    up_a = jnp.dot(xn, wua_ref[...], preferred_element_type=F32)
    fa = _ffn_conv3(up_a, ha_ref, j, cwa_ref, cba_ref, ea_ref, tm)
    up_b = jnp.dot(xn, wub_ref[...], preferred_element_type=F32)
    fb = _ffn_conv3(up_b, hb_ref, j, cwb_ref, cbb_ref, eb_ref, tm)
    ta_ref[...] = ea_ref[tm:tm + 8, :]
    tb_ref[...] = eb_ref[tm:tm + 8, :]
    gact = (_gelu(fa) * fb).astype(BF16)
    acc_ref[...] += jnp.dot(gact, wd_ref[...], preferred_element_type=F32)

    @pl.when(j == nj - 1)
    def _():
        y_ref[...] = acc_ref[...]


def _ffn_prompt(x, g, wu, wd, cw, cb, tm, hc):
    n, s, d = x.shape
    nj = D_FF // hc
    return pl.pallas_call(
        functools.partial(_ffn_prompt_kernel, tm=tm),
        grid=(n, s // tm, nj),
        in_specs=[pl.BlockSpec((None, tm, d), lambda b, i, j: (b, i, 0)),
                  pl.BlockSpec((1, d), lambda b, i, j: (0, 0)),
                  pl.BlockSpec((d, hc), lambda b, i, j: (0, j)),
                  pl.BlockSpec((d, hc), lambda b, i, j: (0, nj + j)),
                  pl.BlockSpec((hc, d), lambda b, i, j: (j, 0)),
                  pl.BlockSpec((FFN_CONV, hc), lambda b, i, j: (0, j)),
                  pl.BlockSpec((FFN_CONV, hc), lambda b, i, j: (0, nj + j)),
                  pl.BlockSpec((1, hc), lambda b, i, j: (0, j)),
                  pl.BlockSpec((1, hc), lambda b, i, j: (0, nj + j))],
        out_specs=[pl.BlockSpec((None, tm, d), lambda b, i, j: (b, i, 0)),
                   pl.BlockSpec((None, None, 8, hc), lambda b, i, j: (b, i, 0, j)),
                   pl.BlockSpec((None, None, 8, hc), lambda b, i, j: (b, i, 0, j))],
        out_shape=[jax.ShapeDtypeStruct((n, s, d), F32),
                   jax.ShapeDtypeStruct((n, s // tm, 8, D_FF), F32),
                   jax.ShapeDtypeStruct((n, s // tm, 8, D_FF), F32)],
        scratch_shapes=[pltpu.VMEM((tm, d), BF16),
                        pltpu.VMEM((tm, d), F32),
                        pltpu.VMEM((nj, 8, hc), F32),
                        pltpu.VMEM((nj, 8, hc), F32),
                        pltpu.VMEM((tm + 8, hc), F32),
                        pltpu.VMEM((tm + 8, hc), F32)],
        compiler_params=_cparams(("parallel", "arbitrary", "arbitrary")),
        name="ffn_prompt",
    )(x, g, wu, wu, wd, cw, cw, cb, cb)


def _state_sample_kernel(z_ref, rh_ref, rbuf_ref, cbuf_ref,
                         rcw_ref, rcb_ref, wa_ref, wx_ref, ba_ref, bx_ref, lam_ref,
                         ccw_ref, ccb_ref, ng_ref, nb_ref,
                         h_ref, c_ref, rnew_ref, cnew_ref):
    x_rnn = z_ref[:, COL_RNN:COL_RNN + D_RNN].astype(F32)
    u = rcb_ref[...] + rcw_ref[RNN_CONV - 1:RNN_CONV, :] * x_rnn
    for k in range(RNN_CONV - 1):
        row = rbuf_ref[k]
        u = u + rcw_ref[k:k + 1, :] * row
        if k > 0:
            rnew_ref[k - 1] = row
    rnew_ref[RNN_CONV - 2] = x_rnn
    ub = u.astype(BF16)
    r_pre = jnp.concatenate(
        [jnp.dot(ub[:, b * RNN_BW:(b + 1) * RNN_BW], wa_ref[b], preferred_element_type=F32)
         for b in range(RNN_BLOCKS)], axis=-1)
    i_pre = jnp.concatenate(
        [jnp.dot(ub[:, b * RNN_BW:(b + 1) * RNN_BW], wx_ref[b], preferred_element_type=F32)
         for b in range(RNN_BLOCKS)], axis=-1)
    a, uu = _lru_coeffs(u, r_pre, i_pre, ba_ref[...], bx_ref[...], _softplus(-lam_ref[...]))
    h_ref[...] = uu + a * rh_ref[...]

    ga = z_ref[:, COL_GLU:COL_GLU + D_CONV].astype(F32)
    gb = z_ref[:, COL_GLU + D_CONV:COL_GLU + 2 * D_CONV].astype(F32)
    c_in = ga * _sigmoid(gb)
    y = ccb_ref[...] + ccw_ref[CCONV - 1:CCONV, :] * c_in
    for k in range(CCONV - 1):
        row = cbuf_ref[k]
        y = y + ccw_ref[k:k + 1, :] * row
        if k > 0:
            cnew_ref[k - 1] = row
    cnew_ref[CCONV - 2] = c_in
    c_ref[...] = _layernorm_swish(y, ng_ref[...], nb_ref[...]).astype(c_ref.dtype)


def _state_sample(z, rh, rbuf, cbuf, rcw, rcb, wa, wx, ba, bx, lam, ccw, ccb, ng, nbias, nb):
    m = z.shape[0]
    wz = COL_GATE
    row = lambda w: pl.BlockSpec((nb, w), lambda i: (i, 0))
    full = lambda a: pl.BlockSpec(a.shape, lambda i: (0,) * a.ndim)
    rspec = pl.BlockSpec((RNN_CONV - 1, nb, D_RNN), lambda i: (0, i, 0))
    cspec = pl.BlockSpec((CCONV - 1, nb, D_CONV), lambda i: (0, i, 0))
    rbuf = rbuf.transpose(1, 0, 2)
    cbuf = cbuf.transpose(1, 0, 2)
    h, c, rnew, cnew = pl.pallas_call(
        _state_sample_kernel,
        grid=(m // nb,),
        in_specs=[row(wz), row(D_RNN), rspec, cspec,
                  full(rcw), full(rcb), full(wa), full(wx), full(ba), full(bx), full(lam),
                  full(ccw), full(ccb), full(ng), full(nbias)],
        out_specs=[row(D_RNN), row(D_CONV), rspec, cspec],
        out_shape=[jax.ShapeDtypeStruct((m, D_RNN), F32),
                   jax.ShapeDtypeStruct((m, D_CONV), BF16),
                   jax.ShapeDtypeStruct((RNN_CONV - 1, m, D_RNN), F32),
                   jax.ShapeDtypeStruct((CCONV - 1, m, D_CONV), F32)],
        compiler_params=_cparams(("parallel",)),
        name="state_sample",
    )(z, rh, rbuf, cbuf, rcw, rcb, wa, wx, ba, bx, lam, ccw, ccb, ng, nbias)
    return h, c, rnew.transpose(1, 0, 2), cnew.transpose(1, 0, 2)


def _attn_sample_kernel(qkv_ref, kv_ref, gq_ref, gk_ref, o_ref, lse_ref, kvn_ref):
    scale = HD ** -0.5
    qn = _rmsnorm_rows(qkv_ref[:, 0], gq_ref[...])
    kn = _rmsnorm_rows(qkv_ref[:, 1], gk_ref[...])
    vn = qkv_ref[:, 2]
    kvn_ref[:, 0] = kn
    kvn_ref[:, 1] = vn
    kc = kv_ref[:, :, 0]
    vc = kv_ref[:, :, 1]
    s_new = jnp.sum(qn * kn, axis=-1, keepdims=True) * scale
    s_buf = jnp.sum(qn[:, None] * kc, axis=-1, keepdims=True) * scale
    m = jnp.maximum(jnp.max(s_buf, axis=1), s_new)
    p_new = jnp.exp(s_new - m)
    p_buf = jnp.exp(s_buf - m[:, None])
    l = p_new + jnp.sum(p_buf, axis=1)
    acc = p_new * vn + jnp.sum(p_buf * vc, axis=1)
    o_ref[...] = acc / l
    lse_ref[...] = jnp.broadcast_to(m + jnp.log(l), lse_ref.shape)


def _attn_sample(qkv, cache, layer, gq, gk, gi, nb):
    m = qkv.shape[0]
    win, dil = ATT_GROUPS[gi]
    depth, mb, L = cache.shape[:3]
    assert L == win and L // dil == ATT_BLK and mb == m
    view = cache.reshape(depth, m, ATT_BLK, dil, 2, H_G, HD)
    hspec = pl.BlockSpec((nb, H_G, HD), lambda i: (i, 0, 0))
    gspec = pl.BlockSpec((1, 1, HD), lambda i: (0, 0, 0))
    return pl.pallas_call(
        _attn_sample_kernel,
        grid=(m // nb,),
        in_specs=[pl.BlockSpec((nb, None, 3, H_G, HD), lambda i: (i, gi, 0, 0, 0)),
                  pl.BlockSpec((None, nb, ATT_BLK, None, 2, H_G, HD),
                               lambda i: (layer, i, 0, 0, 0, 0, 0)),
                  gspec, gspec],
        out_specs=[hspec, hspec, pl.BlockSpec((nb, 2, H_G, HD), lambda i: (i, 0, 0, 0))],
        out_shape=[jax.ShapeDtypeStruct((m, H_G, HD), F32),
                   jax.ShapeDtypeStruct((m, H_G, HD), F32),
                   jax.ShapeDtypeStruct((m, 2, H_G, HD), F32)],
        compiler_params=_cparams(("parallel",)),
        name=f"attn_sample_g{gi}",
    )(qkv, view, gq.reshape(1, 1, HD), gk.reshape(1, 1, HD))


def _ffn_sample_kernel(x_ref, g_ref, buf_ref, wu_ref, wd_ref, cw_ref, cb_ref,
                       y_ref, new_ref, xn_ref, acc_ref):
    j = pl.program_id(0)
    nj = pl.num_programs(0)

    @pl.when(j == 0)
    def _():
        x = x_ref[...]
        xn_ref[...] = _rmsnorm_rows(x, g_ref[...]).astype(BF16)
        acc_ref[...] = x

    f = []
    for half in range(2):
        up = jnp.dot(xn_ref[...], wu_ref[half], preferred_element_type=F32)
        b0 = buf_ref[0, half]
        b1 = buf_ref[1, half]
        y = cb_ref[half] + cw_ref[half, FFN_CONV - 1:FFN_CONV, :] * up
        y = y + cw_ref[half, 0:1, :] * b0 + cw_ref[half, 1:2, :] * b1
        new_ref[0, half] = b1
        new_ref[1, half] = up
        f.append(y)
    gact = (_gelu(f[0]) * f[1]).astype(BF16)
    acc_ref[...] += jnp.dot(gact, wd_ref[...], preferred_element_type=F32)

    @pl.when(j == nj - 1)
    def _():
        y_ref[...] = acc_ref[...]


def _ffn_sample(x, g, buf, wu, wd, cw, cb, hc):
    m, d = x.shape
    nj = D_FF // hc
    buf4 = buf.reshape(m, FFN_CONV - 1, 2, D_FF).transpose(1, 2, 0, 3)
    wu3 = wu.reshape(d, 2, D_FF).transpose(1, 0, 2)
    cw3 = cw.reshape(FFN_CONV, 2, D_FF).transpose(1, 0, 2)
    cb3 = cb.reshape(2, 1, D_FF)
    bspec = pl.BlockSpec((FFN_CONV - 1, 2, m, hc), lambda j: (0, 0, 0, j))
    y, new = pl.pallas_call(
        _ffn_sample_kernel,
        grid=(nj,),
        in_specs=[pl.BlockSpec((m, d), lambda j: (0, 0)),
                  pl.BlockSpec((1, d), lambda j: (0, 0)),
                  bspec,
                  pl.BlockSpec((2, d, hc), lambda j: (0, 0, j)),
                  pl.BlockSpec((hc, d), lambda j: (j, 0)),
                  pl.BlockSpec((2, FFN_CONV, hc), lambda j: (0, 0, j)),
                  pl.BlockSpec((2, 1, hc), lambda j: (0, 0, j))],
        out_specs=[pl.BlockSpec((m, d), lambda j: (0, 0)), bspec],
        out_shape=[jax.ShapeDtypeStruct((m, d), F32),
                   jax.ShapeDtypeStruct((FFN_CONV - 1, 2, m, D_FF), F32)],
        scratch_shapes=[pltpu.VMEM((m, d), BF16), pltpu.VMEM((m, d), F32)],
        compiler_params=_cparams(("arbitrary",)),
        name="ffn_sample",
    )(x, g, buf4, wu3, wd, cw3, cb3)
    return y, new.transpose(2, 0, 1, 3).reshape(m, FFN_CONV - 1, 2 * D_FF)


def _prep_layer_weights(lw):
    w_in = lw['w_in']
    o1 = D_RNN
    o2 = o1 + N_GROUPS * 3 * ATT_W
    o3 = o2 + 2 * D_CONV
    w_perm = jnp.concatenate([w_in[:, o2:o3], w_in[:, :o1], w_in[:, o3:], w_in[:, o1:o2]], axis=1)
    row = lambda v: v.reshape(1, -1)
    return dict(
        norm1_g=row(lw['norm1_g']), w_in=w_perm.astype(BF16),
        rnn_conv_w=lw['rnn_conv_w'], rnn_conv_b=row(lw['rnn_conv_b']),
        rnn_wa=lw['rnn_wa'].astype(BF16), rnn_wx=lw['rnn_wx'].astype(BF16),
        rnn_ba=row(lw['rnn_ba']), rnn_bx=row(lw['rnn_bx']), rnn_lambda=row(lw['rnn_lambda']),
        q_norm_g=lw['q_norm_g'], k_norm_g=lw['k_norm_g'],
        cconv_w=lw['cconv_w'], cconv_b=row(lw['cconv_b']),
        cnorm_g=row(lw['cnorm_g']), cnorm_b=row(lw['cnorm_b']),
        w_br_rnn=lw['w_br_rnn'].astype(BF16), w_br_attn=lw['w_br_attn'].astype(BF16),
        w_br_conv=lw['w_br_conv'].astype(BF16), w_o=lw['w_o'].astype(BF16),
        norm2_g=row(lw['norm2_g']), ffn_up=lw['ffn_up'].astype(BF16),
        ffn_conv_w=lw['ffn_conv_w'], ffn_conv_b=row(lw['ffn_conv_b']),
        ffn_down=lw['ffn_down'].astype(BF16))


def _prompt_layer(x, w, tiles):
    n, s, d = x.shape
    t = n * s
    z2d, zqkv = _inproj(x.reshape(t, d), w['norm1_g'], w['w_in'], tiles['in_tm'], tiles['in_tn'])
    z = z2d.reshape(n, s, COL_QKV)
    zqkv = zqkv.reshape(N_GROUPS * 3, H_G, n, s, HD)
    h_seq, h_last = _rnn_prompt(z, w['rnn_conv_w'], w['rnn_conv_b'], w['rnn_wa'], w['rnn_wx'],
                                w['rnn_ba'], w['rnn_bx'], w['rnn_lambda'], tiles['rnn_tm'])
    o_list, lse_list, kv_new = [], [], []
    for gi, (win, dil) in enumerate(ATT_GROUPS):
        o, lse, kn = _attn_prompt(zqkv, w['q_norm_g'][gi:gi + 1], w['k_norm_g'][gi:gi + 1], gi,
                                  tiles['att_tt'])
        o_list.append(o.reshape(H_G, t, HD))
        lse_list.append(lse.reshape(t, HD))
        k_tail = kn.transpose(1, 2, 0, 3)
        v_tail = zqkv[3 * gi + 2, :, :, s - win:].transpose(1, 2, 0, 3)
        kv_new.append(jnp.stack([k_tail, v_tail], axis=2))
    c, c_tail = _cconv_prompt(z, w['cconv_w'], w['cconv_b'], w['cnorm_g'], w['cnorm_b'],
                              tiles['cc_tm'])
    x1 = _merge(x.reshape(t, d), z2d, h_seq.reshape(t, D_RNN), o_list, lse_list,
                c.reshape(t, D_CONV), w['w_br_rnn'], w['w_br_attn'], w['w_br_conv'], w['w_o'],
                tiles['mrg_tm'], True)
    x2, ta, tb = _ffn_prompt(x1.reshape(n, s, d), w['norm2_g'], w['ffn_up'], w['ffn_down'],
                             w['ffn_conv_w'], w['ffn_conv_b'], tiles['ffn_tm'], tiles['ffn_hc'])
    rnn_buf_new = z[:, s - (RNN_CONV - 1):, COL_RNN:COL_RNN + D_RNN].astype(F32)
    cconv_new = c_tail[:, CC_PAD - (CCONV - 1):]
    ffn_new = jnp.concatenate([ta[:, -1, 8 - (FFN_CONV - 1):], tb[:, -1, 8 - (FFN_CONV - 1):]],
                              axis=-1)
    return x2, kv_new, h_last.reshape(n, D_RNN), rnn_buf_new, cconv_new, ffn_new


def _sample_layer(x, w, layer, rnn_h, rnn_buf, cconv_buf, ffn_buf, caches, tiles):
    m, _, d = x.shape
    x2d = x.reshape(m, d)
    z, zqkv = _inproj(x2d, w['norm1_g'], w['w_in'], m, tiles['in_tn'])
    h, c, rnew, cnew = _state_sample(
        z, rnn_h, rnn_buf, cconv_buf, w['rnn_conv_w'], w['rnn_conv_b'], w['rnn_wa'], w['rnn_wx'],
        w['rnn_ba'], w['rnn_bx'], w['rnn_lambda'], w['cconv_w'], w['cconv_b'],
        w['cnorm_g'], w['cnorm_b'], tiles['smp_state_nb'])
    qkv = zqkv.transpose(1, 0, 2).reshape(m, N_GROUPS, 3, H_G, HD)
    o_list, lse_list, kv_new = [], [], []
    for gi in range(N_GROUPS):
        o, lse, kvn = _attn_sample(qkv, caches[gi], layer, w['q_norm_g'][gi], w['k_norm_g'][gi],
                                   gi, tiles['smp_attn_nb'])
        o_list.append(o.transpose(1, 0, 2))
        lse_list.append(jnp.pad(lse[:, :, 0], ((0, 0), (0, HD - H_G))))
        kv_new.append(kvn.reshape(m, 1, 2, H_G, HD))
    x1 = _merge(x2d, z, h.astype(BF16), o_list, lse_list, c,
                w['w_br_rnn'], w['w_br_attn'], w['w_br_conv'], w['w_o'], m, False)
    x2, ffn_new = _ffn_sample(x1, w['norm2_g'], ffn_buf, w['ffn_up'], w['ffn_down'],
                              w['ffn_conv_w'], w['ffn_conv_b'], tiles['ffn_hc'])
    return x2.reshape(m, 1, d), kv_new, h, rnew, cnew, ffn_new


def _tiles(s):
    return dict(in_tm=min(1024, s), in_tn=1536, rnn_tm=min(1024, s), att_tt=2048,
                cc_tm=min(1024, s), mrg_tm=min(512, s), ffn_tm=min(1024, s), ffn_hc=1024,
                smp_state_nb=32, smp_attn_nb=8)


def kernel(x_prompt, x_sample, cache_kv_w128, cache_kv_w512, cache_kv_w2048, state_rnn_h, state_rnn_conv, state_cconv, state_ffn_conv, norm1_g, w_in, rnn_conv_w, rnn_conv_b, rnn_wa, rnn_ba, rnn_wx, rnn_bx, rnn_lambda, q_norm_g, k_norm_g, cconv_w, cconv_b, cnorm_g, cnorm_b, w_br_rnn, w_br_attn, w_br_conv, w_o, norm2_g, ffn_up, ffn_conv_w, ffn_conv_b, ffn_down):
    params = dict(norm1_g=norm1_g, w_in=w_in, rnn_conv_w=rnn_conv_w, rnn_conv_b=rnn_conv_b,
                  rnn_wa=rnn_wa, rnn_ba=rnn_ba, rnn_wx=rnn_wx, rnn_bx=rnn_bx,
                  rnn_lambda=rnn_lambda, q_norm_g=q_norm_g, k_norm_g=k_norm_g, cconv_w=cconv_w,
                  cconv_b=cconv_b, cnorm_g=cnorm_g, cnorm_b=cnorm_b, w_br_rnn=w_br_rnn,
                  w_br_attn=w_br_attn, w_br_conv=w_br_conv, w_o=w_o, norm2_g=norm2_g,
                  ffn_up=ffn_up, ffn_conv_w=ffn_conv_w, ffn_conv_b=ffn_conv_b, ffn_down=ffn_down)
    depth = w_in.shape[0]
    assert x_sample.shape[1] == 1
    caches = (cache_kv_w128, cache_kv_w512, cache_kv_w2048)
    tiles = _tiles(x_prompt.shape[1])
    xp, xs = x_prompt, x_sample
    pk, sk = [[], [], []], [[], [], []]
    p_state = [[], [], [], []]
    s_state = [[], [], [], []]
    for l in range(depth):
        w = _prep_layer_weights({k: v[l] for k, v in params.items()})
        xp, kvp, hp, rbp, cbp, fbp = _prompt_layer(xp, w, tiles)
        xs, kvs, hs, rbs, cbs, fbs = _sample_layer(
            xs, w, l, state_rnn_h[l], state_rnn_conv[l], state_cconv[l], state_ffn_conv[l],
            caches, tiles)
        for gi in range(N_GROUPS):
            pk[gi].append(kvp[gi])
            sk[gi].append(kvs[gi])
        for lst, v in zip(p_state, (hp, rbp, cbp, fbp)):
            lst.append(v)
        for lst, v in zip(s_state, (hs, rbs, cbs, fbs)):
            lst.append(v)
    stack = lambda xs_: jnp.stack(xs_, 0)
    return (xp, xs, stack(pk[0]), stack(pk[1]), stack(pk[2]),
            stack(p_state[0]), stack(p_state[1]), stack(p_state[2]), stack(p_state[3]),
            stack(sk[0]), stack(sk[1]), stack(sk[2]),
            stack(s_state[0]), stack(s_state[1]), stack(s_state[2]), stack(s_state[3]))
```

```python
import functools

import jax
import jax.numpy as jnp
from jax import lax
from jax.experimental import pallas as pl
from jax.experimental.pallas import tpu as pltpu

F32 = jnp.float32
BF16 = jnp.bfloat16

D_MODEL = 1024
D_RNN = D_MODEL
RNN_BLOCKS = 8
RNN_BW = D_RNN // RNN_BLOCKS
RNN_CONV = 4
LRU_C = 8.0
ATT_GROUPS = ((128, 1), (512, 4), (2048, 16))
N_GROUPS = len(ATT_GROUPS)
H_G = 4
HD = 128
ATT_W = H_G * HD
ATT_BLK = 128
D_CONV = D_MODEL
CCONV = 31
D_FF = 3 * D_MODEL
FFN_CONV = 3
N_BRANCH = 3
EPS = 1e-6
NEG = -1e30

COL_GLU = 0
COL_RNN = 2 * D_CONV
COL_GATE = COL_RNN + D_RNN
COL_QKV = COL_GATE + N_BRANCH * D_MODEL
N_IN = COL_QKV + N_GROUPS * 3 * ATT_W

VMEM_LIMIT = 56 * 1024 * 1024


def _cparams(sem):
    return pltpu.CompilerParams(dimension_semantics=sem, vmem_limit_bytes=VMEM_LIMIT)


def _sigmoid(x):
    return 0.5 * jnp.tanh(0.5 * x) + 0.5


def _softplus(x):
    return jnp.maximum(x, 0.0) + jnp.log1p(jnp.exp(-jnp.abs(x)))


def _rmsnorm_rows(x, g):
    ms = jnp.mean(x * x, axis=-1, keepdims=True)
    return x * lax.rsqrt(ms + EPS) * g


def _layernorm_swish(y, g, b):
    mu = jnp.mean(y, axis=-1, keepdims=True)
    yc = y - mu
    var = jnp.mean(yc * yc, axis=-1, keepdims=True)
    yn = yc * lax.rsqrt(var + EPS) * g + b
    return yn * _sigmoid(yn)


def _inproj_kernel(x_ref, g_ref, w_ref, o_ref, qkv_ref, xn_ref, *, n_main):
    j = pl.program_id(1)

    @pl.when(j == 0)
    def _():
        xn_ref[...] = _rmsnorm_rows(x_ref[...], g_ref[...]).astype(BF16)

    res = jnp.dot(xn_ref[...], w_ref[...], preferred_element_type=F32)

    @pl.when(j < n_main)
    def _():
        o_ref[...] = res.astype(o_ref.dtype)

    @pl.when(j >= n_main)
    def _():
        for c in range(qkv_ref.shape[0]):
            qkv_ref[c] = res[:, c * HD:(c + 1) * HD]


def _inproj(x2d, g, w, tm, tn):
    t, d = x2d.shape
    n_main = COL_QKV // tn
    n_qkv = (N_IN - COL_QKV) // tn
    hpt = tn // HD
    return pl.pallas_call(
        functools.partial(_inproj_kernel, n_main=n_main),
        grid=(t // tm, n_main + n_qkv),
        in_specs=[pl.BlockSpec((tm, d), lambda i, j: (i, 0)),
                  pl.BlockSpec((1, d), lambda i, j: (0, 0)),
                  pl.BlockSpec((d, tn), lambda i, j: (0, j))],
        out_specs=[pl.BlockSpec((tm, tn), lambda i, j: (i, jnp.minimum(j, n_main - 1))),
                   pl.BlockSpec((hpt, tm, HD), lambda i, j: (jnp.maximum(j - n_main, 0), i, 0))],
        out_shape=[jax.ShapeDtypeStruct((t, COL_QKV), BF16),
                   jax.ShapeDtypeStruct((n_qkv * hpt, t, HD), F32)],
        scratch_shapes=[pltpu.VMEM((tm, d), BF16)],
        compiler_params=_cparams(("parallel", "arbitrary")),
        name="inproj",
    )(x2d, g, w)


def _lru_coeffs(u, r_pre, i_pre, ba, bx, sp):
    r = _sigmoid(r_pre + ba)
    ig = _sigmoid(i_pre + bx)
    log_a = (-LRU_C * r) * sp
    a = jnp.exp(log_a)
    mult = jnp.sqrt(-jnp.tanh(log_a) * (a * a + 1.0))
    return a, mult * (ig * u)


RNN_RC = 256


def _rnn_prompt_kernel(x_ref, cw_ref, cb_ref, wa_ref, wx_ref, ba_ref, bx_ref, lam_ref,
                       h_ref, hlast_ref, ext_ref, hp_ref, hcar_ref, *, tm):
    i = pl.program_id(1)
    nch = RNN_RC // 64
    rpv = RNN_RC // 8

    @pl.when(i == 0)
    def _():
        ext_ref[:, 0:8, :] = jnp.zeros((RNN_BLOCKS, 8, RNN_BW), F32)
        hcar_ref[...] = jnp.zeros((1, D_RNN), F32)

    for b in range(RNN_BLOCKS):
        ext_ref[b, 8:8 + tm, :] = x_ref[:, b * RNN_BW:(b + 1) * RNN_BW].astype(F32)

    row = lax.broadcasted_iota(jnp.int32, (8, RNN_BW), 0)

    for b in range(RNN_BLOCKS):
        cs = slice(b * RNN_BW, (b + 1) * RNN_BW)
        sp = _softplus(-lam_ref[:, cs])
        cb = cb_ref[:, cs]
        ba = ba_ref[:, cs]
        bx = bx_ref[:, cs]
        cw = [cw_ref[k:k + 1, cs] for k in range(RNN_CONV)]
        wa = wa_ref[b]
        wx = wx_ref[b]

        def gate_chunk(c, car, b=b, cs=cs, sp=sp, cb=cb, ba=ba, bx=bx, cw=cw, wa=wa, wx=wx):
            r0 = pl.multiple_of(c * RNN_RC, RNN_RC)
            slabs = [ext_ref[b, pl.ds(r0 + (8 - (RNN_CONV - 1) + wv), rpv, stride=8), :]
                     for wv in range(8 + RNN_CONV - 1)]
            us = []
            for v in range(8):
                u = cb
                for k in range(RNN_CONV):
                    u = u + cw[k] * slabs[v + k]
                us.append(u)
            u_all = jnp.concatenate(us, axis=0)
            ub = u_all.astype(BF16)
            r_pre = jnp.dot(ub, wa, preferred_element_type=F32)
            i_pre = jnp.dot(ub, wx, preferred_element_type=F32)
            a_all, uu_all = _lru_coeffs(u_all, r_pre, i_pre, ba, bx, sp)
            hs, ps = [], []
            for v in range(8):
                a = a_all[v * rpv:(v + 1) * rpv]
                uu = uu_all[v * rpv:(v + 1) * rpv]
                h = uu if v == 0 else a * h + uu
                p = a if v == 0 else a * p
                hs.append(h)
                ps.append(p)
            cins = []
            for q in range(nch):
                pe, he = _sublane_scan(p[8 * q:8 * q + 8], h[8 * q:8 * q + 8], row)
                e = he + pe * car
                cins.append(jnp.where(row == 0, car, pltpu.roll(e, 1, axis=0)))
                car = e[7:8, :]
            cin = jnp.concatenate(cins, axis=0)
            for v in range(8):
                hp_ref[b, v * rpv:(v + 1) * rpv, :] = hs[v] + ps[v] * cin
            for j in range(0, rpv, 2):
                t0 = hp_ref[b, pl.ds(j, 8, stride=rpv), :]
                t1 = hp_ref[b, pl.ds(j + 1, 8, stride=rpv), :]
                dst = pl.multiple_of(r0 + 8 * j, 16)
                h_ref[pl.ds(dst, 16), cs] = jnp.concatenate([t0, t1], axis=0).astype(h_ref.dtype)
            return car

        car = lax.fori_loop(0, tm // RNN_RC, gate_chunk, hcar_ref[:, cs])
        hcar_ref[:, cs] = car
        hlast_ref[:, cs] = car
        ext_ref[b, 0:8, :] = ext_ref[b, tm:tm + 8, :]


def _sublane_scan(p, h, row):
    for d in (1, 2, 4):
        hr = pltpu.roll(h, d, axis=0)
        pr = pltpu.roll(p, d, axis=0)
        ok = row >= d
        h = jnp.where(ok, h + p * hr, h)
        p = jnp.where(ok, p * pr, p)
    return p, h


def _rnn_prompt(z, cw, cb, wa, wx, ba, bx, lam, tm):
    n, s, _ = z.shape
    assert tm % RNN_RC == 0
    cblk = 0
    vec = pl.BlockSpec((1, D_RNN), lambda b, i: (0, 0))
    blkw = pl.BlockSpec((RNN_BLOCKS, RNN_BW, RNN_BW), lambda b, i: (0, 0, 0))
    return pl.pallas_call(
        functools.partial(_rnn_prompt_kernel, tm=tm),
        grid=(n, s // tm),
        in_specs=[pl.BlockSpec((None, tm, D_RNN), lambda b, i: (b, i, cblk)),
                  pl.BlockSpec((RNN_CONV, D_RNN), lambda b, i: (0, 0)),
                  vec, blkw, blkw, vec, vec, vec],
        out_specs=[pl.BlockSpec((None, tm, D_RNN), lambda b, i: (b, i, 0)),
                   pl.BlockSpec((None, 1, D_RNN), lambda b, i: (b, 0, 0))],
        out_shape=[jax.ShapeDtypeStruct((n, s, D_RNN), BF16),
                   jax.ShapeDtypeStruct((n, 1, D_RNN), F32)],
        scratch_shapes=[pltpu.VMEM((RNN_BLOCKS, tm + 8, RNN_BW), F32),
                        pltpu.VMEM((RNN_BLOCKS, RNN_RC, RNN_BW), F32),
                        pltpu.VMEM((1, D_RNN), F32)],
        compiler_params=_cparams(("parallel", "arbitrary")),
        name="rnn_prompt",
    )(z, cw, cb, wa, wx, ba, bx, lam)


ATT_UNROLL = 2


def _attn_prompt_kernel(q_ref, k_ref, v_ref, gq_ref, gk_ref,
                        o_ref, lse_ref, kn_ref, ks_ref, vs_ref, knt_ref, *, dil, tt, nblk_total):
    ti = pl.program_id(1)
    nblk = tt // (ATT_BLK * dil)
    units = tt // ATT_BLK
    sb = tt // nblk

    @pl.when(ti == 0)
    def _():
        ks_ref[...] = jnp.zeros(ks_ref.shape, BF16)
        vs_ref[...] = jnp.zeros(vs_ref.shape, BF16)

    sa = lax.broadcasted_iota(jnp.int32, (ATT_BLK, ATT_BLK), 0)
    sk = lax.broadcasted_iota(jnp.int32, (ATT_BLK, ATT_BLK), 1)
    if dil == 4:
        half = ATT_BLK // 2
        pos = lambda a: 2 * (a & (half - 1)) + (a >> (half.bit_length() - 1))
    else:
        pos = lambda a: a
    own_ok = pos(sk) <= pos(sa)
    prev_band = pos(sk) >= pos(sa)
    lane = lax.broadcasted_iota(jnp.int32, (ATT_BLK, HD), 1)
    ones = jnp.ones((ATT_BLK, HD), BF16)
    gq = gq_ref[...] * (HD ** -0.5)
    gk = gk_ref[...]
    dn = (((1,), (1,)), ((), ()))
    upb = units // nblk
    sh = upb.bit_length() - 1

    def load(ref, h, jb, r):
        base = jb * sb + r
        if dil == 1:
            return ref[h, pl.ds(pl.multiple_of(base, ATT_BLK), ATT_BLK), :]
        if dil == 4:
            return jnp.concatenate([ref[h, pl.ds(base, ATT_BLK // 2, stride=8), :],
                                    ref[h, pl.ds(base + 4, ATT_BLK // 2, stride=8), :]], axis=0)
        return ref[h, pl.ds(base, ATT_BLK, stride=dil), :]

    def store_natural(ref, h, r, val):
        if dil == 1:
            ref[h] = val
        elif dil == 4:
            ref[h, pl.ds(r, ATT_BLK // 2, stride=8), :] = val[:ATT_BLK // 2]
            ref[h, pl.ds(r + 4, ATT_BLK // 2, stride=8), :] = val[ATT_BLK // 2:]
        else:
            ref[h, pl.ds(r, ATT_BLK, stride=dil), :] = val

    def unit_group(g, carry):
        us = [g * ATT_UNROLL + e for e in range(ATT_UNROLL)]
        jb = [u >> sh for u in us]
        r = [u & (upb - 1) for u in us]
        gblk = [ti * nblk + j for j in jb]
        par = [gb & 1 for gb in gblk]
        prev_ok = [jnp.logical_and(prev_band, gb > 0) for gb in gblk]
        dst = [_unit_rows(dil, jb[e], sb, r[e]) for e in range(ATT_UNROLL)]
        eh = [(e, h) for e in range(ATT_UNROLL) for h in range(H_G)]
        qb = {k: _rmsnorm_rows(load(q_ref, k[1], jb[k[0]], r[k[0]]), gq).astype(BF16) for k in eh}
        kn = {k: _rmsnorm_rows(load(k_ref, k[1], jb[k[0]], r[k[0]]), gk) for k in eh}
        for e, h in eh:
            knt_ref[e, h] = kn[e, h]
        kb = {k: kn[k].astype(BF16) for k in eh}
        vb = {k: jnp.concatenate([load(v_ref, k[1], jb[k[0]], r[k[0]]).astype(BF16), ones], axis=1)
              for k in eh}
        kprev, vprev = {}, {}
        for e, h in eh:
            if upb == 1 and e > 0:
                kprev[e, h], vprev[e, h] = kb[e - 1, h], vb[e - 1, h]
            else:
                kprev[e, h] = ks_ref[r[e], h, 1 - par[e]]
                vprev[e, h] = vs_ref[r[e], h, 1 - par[e]]
        for e, h in eh:
            ks_ref[r[e], h, par[e]] = kb[e, h]
            vs_ref[r[e], h, par[e]] = vb[e, h]
        s = {k: lax.dot_general(qb[k], jnp.concatenate([kprev[k], kb[k]], axis=0), dn,
                                preferred_element_type=F32) for k in eh}
        s_prev = {k: jnp.where(prev_ok[k[0]], s[k][:, :ATT_BLK], NEG) for k in eh}
        s_own = {k: jnp.where(own_ok, s[k][:, ATT_BLK:], NEG) for k in eh}
        m = {k: jnp.max(jnp.maximum(s_prev[k], s_own[k]), axis=-1, keepdims=True) for k in eh}
        p = {k: jnp.concatenate([jnp.exp(s_prev[k] - m[k]), jnp.exp(s_own[k] - m[k])],
                                axis=1).astype(BF16) for k in eh}
        acc = {k: jnp.dot(p[k], jnp.concatenate([vprev[k], vb[k]], axis=0),
                          preferred_element_type=F32) for k in eh}
        for e in range(ATT_UNROLL):
            lse_tile = jnp.zeros((ATT_BLK, HD), F32)
            for h in range(H_G):
                l = acc[e, h][:, HD:]
                o = acc[e, h][:, :HD] / l
                for lo, n_rows, src in dst[e]:
                    o_ref[h, pl.ds(lo, n_rows), :] = o[src:src + n_rows]
                lse_tile = jnp.where(lane == h, m[e, h] + jnp.log(l), lse_tile)
            for lo, n_rows, src in dst[e]:
                lse_ref[pl.ds(lo, n_rows), :] = lse_tile[src:src + n_rows]

        for e in range(ATT_UNROLL):
            @pl.when(gblk[e] == nblk_total - 1)
            def _(e=e):
                for h in range(H_G):
                    store_natural(kn_ref, h, r[e], knt_ref[e, h])

        return carry

    lax.fori_loop(0, units // ATT_UNROLL, unit_group, 0)


def _unit_rows(dil, jb, sb, r):
    base = jb * sb
    if dil == 1:
        return [(pl.multiple_of(base, ATT_BLK), ATT_BLK, 0)]
    if dil == 4:
        half = ATT_BLK // 2
        return [(pl.multiple_of(base + half * r, half), half, 0),
                (pl.multiple_of(base + sb // 2 + half * r, half), half, half)]
    return [(pl.multiple_of(base + ATT_BLK * r, ATT_BLK), ATT_BLK, 0)]


def _attn_prompt(zqkv, gq, gk, gi, tt):
    _, _, n, s, _ = zqkv.shape
    win, dil = ATT_GROUPS[gi]
    assert win == ATT_BLK * dil and tt % win == 0 and s % tt == 0
    upb = tt // ATT_BLK // (tt // win)
    spec = lambda c: pl.BlockSpec((None, H_G, None, tt, HD), lambda b, i, c=c: (c, 0, b, i, 0))
    gspec = pl.BlockSpec((1, HD), lambda b, i: (0, 0))
    hspec = pl.BlockSpec((H_G, None, tt, HD), lambda b, i: (0, b, i, 0))
    return pl.pallas_call(
        functools.partial(_attn_prompt_kernel, dil=dil, tt=tt, nblk_total=s // win),
        grid=(n, s // tt),
        in_specs=[spec(3 * gi), spec(3 * gi + 1), spec(3 * gi + 2), gspec, gspec],
        out_specs=[hspec, pl.BlockSpec((None, tt, HD), lambda b, i: (b, i, 0)),
                   pl.BlockSpec((H_G, None, win, HD), lambda b, i: (0, b, 0, 0))],
        out_shape=[jax.ShapeDtypeStruct((H_G, n, s, HD), F32),
                   jax.ShapeDtypeStruct((n, s, HD), F32),
                   jax.ShapeDtypeStruct((H_G, n, win, HD), F32)],
        scratch_shapes=[pltpu.VMEM((upb, H_G, 2, ATT_BLK, HD), BF16),
                        pltpu.VMEM((upb, H_G, 2, ATT_BLK, 2 * HD), BF16),
                        pltpu.VMEM((ATT_UNROLL, H_G, ATT_BLK, HD), F32)],
        compiler_params=_cparams(("parallel", "arbitrary")),
        name=f"attn_prompt_g{gi}",
    )(zqkv, zqkv, zqkv, gq, gk)


CC_PAD = 32
CC_RC = 64
CC_LB = 128


FIN_TN = 1536
FIN_PW = 256
FIN_CONV_STEPS = ((0, 3), (3, 6), (6, 9), (9, 12), (12, 16))


def _fused_inproj_kernel(x_ref, g_ref, w_ref, ccw_ref, ccb_ref, ng_ref, nb_ref,
                         rnn_ref, gate_ref, qkv_ref, c_ref, tail_ref,
                         xn_ref, ga_ref, ext_ref, y_ref, *, tm):
    i = pl.program_id(1)
    j = pl.program_id(2)
    npc = FIN_TN // FIN_PW
    nlb = D_CONV // CC_LB
    nv = CC_RC // 8
    off = CC_PAD - (CCONV - 1)

    def piece(p):
        return jnp.dot(xn_ref[...], w_ref[:, p * FIN_PW:(p + 1) * FIN_PW],
                       preferred_element_type=F32)

    def run(sink, units):
        per = -(-len(units) // npc)
        for p in range(npc):
            sink(p, piece(p))
            for unit in units[p * per:(p + 1) * per]:
                unit()

    def glu_to_ext(lb, gb):
        ls = slice(lb * CC_LB, (lb + 1) * CC_LB)
        ext_ref[lb, CC_PAD:CC_PAD + tm, :] = ga_ref[:, ls] * _sigmoid(gb)

    def conv_unit(q, lb):
        ls = slice(lb * CC_LB, (lb + 1) * CC_LB)
        r0 = q * CC_RC
        slabs = {}

        def slab(wv):
            if wv not in slabs:
                slabs[wv] = ext_ref[lb, pl.ds(r0 + off + wv, 8, stride=nv), :]
            return slabs[wv]

        acc = [jnp.zeros((8, CC_LB), F32) + ccb_ref[:, ls]] * nv
        for k in range(CCONV):
            wk = ccw_ref[k:k + 1, ls]
            for v in range(nv):
                acc[v] = acc[v] + wk * slab(v + k)
        y_ref[q % 2, lb] = jnp.concatenate(acc, axis=0)

    def norm_unit(q):
        y = jnp.concatenate(
            [jnp.concatenate([y_ref[q % 2, lb, pl.ds(t, 8, stride=nv), :] for t in range(nv)], axis=0)
             for lb in range(nlb)], axis=-1)
        c_ref[q * CC_RC:(q + 1) * CC_RC, :] = _layernorm_swish(
            y, ng_ref[...], nb_ref[...]).astype(c_ref.dtype)

    def conv_units(q_lo, q_hi):
        units = []
        for q in range(q_lo, q_hi):
            units += [functools.partial(conv_unit, q, lb) for lb in range(nlb)]
            units.append(functools.partial(norm_unit, q))
        return units

    @pl.when(j == 0)
    def _():
        xn_ref[...] = _rmsnorm_rows(x_ref[...], g_ref[...]).astype(BF16)

        @pl.when(i == 0)
        def _():
            ext_ref[:, 0:CC_PAD, :] = jnp.zeros((nlb, CC_PAD, CC_LB), F32)

        def sink(p, val):
            if p * FIN_PW < D_CONV:
                ga_ref[:, p * FIN_PW:(p + 1) * FIN_PW] = val
            else:
                for h in range(FIN_PW // CC_LB):
                    lb = (p * FIN_PW - D_CONV) // CC_LB + h
                    glu_to_ext(lb, val[:, h * CC_LB:(h + 1) * CC_LB])

        run(sink, [])

    @pl.when(j == 1)
    def _():
        n_gb = (2 * D_CONV - FIN_TN) // FIN_PW

        def sink(p, val):
            if p < n_gb:
                for h in range(FIN_PW // CC_LB):
                    lb = (FIN_TN - D_CONV + p * FIN_PW) // CC_LB + h
                    glu_to_ext(lb, val[:, h * CC_LB:(h + 1) * CC_LB])
            else:
                c0 = (p - n_gb) * FIN_PW
                rnn_ref[:, c0:c0 + FIN_PW] = val.astype(rnn_ref.dtype)

        run(sink, [])
        for lb in range(nlb):
            tail_ref[:, lb * CC_LB:(lb + 1) * CC_LB] = ext_ref[lb, tm:tm + CC_PAD, :]

    def gate_sink(p, val):
        gate_ref[:, p * FIN_PW:(p + 1) * FIN_PW] = val.astype(gate_ref.dtype)

    def qkv_sink(p, val):
        for h in range(FIN_PW // HD):
            qkv_ref[p * (FIN_PW // HD) + h] = val[:, h * HD:(h + 1) * HD]

    for step, (q_lo, q_hi) in enumerate(FIN_CONV_STEPS):
        @pl.when(j == 2 + step)
        def _(step=step, q_lo=q_lo, q_hi=q_hi):
            run(gate_sink if step < 2 else qkv_sink, conv_units(q_lo, q_hi))

    @pl.when(j == pl.num_programs(2) - 1)
    def _():
        for lb in range(nlb):
            ext_ref[lb, 0:CC_PAD, :] = ext_ref[lb, tm:tm + CC_PAD, :]


def _fused_inproj(x, g, w, ccw, ccb, ng, nb, tm):
    n, s, d = x.shape
    nj = N_IN // FIN_TN
    assert (COL_RNN + D_RNN, COL_GATE, COL_QKV) == (2 * FIN_TN, 2 * FIN_TN, 4 * FIN_TN)
    assert nj == 2 + len(FIN_CONV_STEPS) and tm == FIN_CONV_STEPS[-1][1] * CC_RC
    hpt = FIN_TN // HD
    vec = pl.BlockSpec((1, D_CONV), lambda b, i, j: (0, 0))
    return pl.pallas_call(
        functools.partial(_fused_inproj_kernel, tm=tm),
        grid=(n, s // tm, nj),
        in_specs=[pl.BlockSpec((None, tm, d), lambda b, i, j: (b, i, 0)),
                  pl.BlockSpec((1, d), lambda b, i, j: (0, 0)),
                  pl.BlockSpec((d, FIN_TN), lambda b, i, j: (0, j)),
                  pl.BlockSpec((CCONV, D_CONV), lambda b, i, j: (0, 0)),
                  vec, vec, vec],
        out_specs=[pl.BlockSpec((None, tm, D_RNN), lambda b, i, j: (b, i, 0)),
                   pl.BlockSpec((None, tm, FIN_TN), lambda b, i, j: (b, i, jnp.clip(j - 2, 0, 1))),
                   pl.BlockSpec((hpt, None, tm, HD), lambda b, i, j: (jnp.clip(j - 4, 0, 2), b, i, 0)),
                   pl.BlockSpec((None, tm, D_CONV), lambda b, i, j: (b, i, 0)),
                   pl.BlockSpec((None, CC_PAD, D_CONV), lambda b, i, j: (b, 0, 0))],
        out_shape=[jax.ShapeDtypeStruct((n, s, D_RNN), BF16),
                   jax.ShapeDtypeStruct((n, s, N_BRANCH * D_MODEL), BF16),
                   jax.ShapeDtypeStruct((N_GROUPS * 3 * H_G, n, s, HD), F32),
                   jax.ShapeDtypeStruct((n, s, D_CONV), BF16),
                   jax.ShapeDtypeStruct((n, CC_PAD, D_CONV), F32)],
        scratch_shapes=[pltpu.VMEM((tm, d), BF16),
                        pltpu.VMEM((tm, D_CONV), F32),
                        pltpu.VMEM((D_CONV // CC_LB, tm + CC_PAD, CC_LB), F32),
                        pltpu.VMEM((2, D_CONV // CC_LB, CC_RC, CC_LB), F32)],
        compiler_params=_cparams(("parallel", "arbitrary", "arbitrary")),
        name="inproj_cconv",
    )(x, g, w, ccw, ccb, ng, nb)


MERGE_CB = 512


def _merge_kernel(x_ref, g_ref, h_ref, o0_ref, o1_ref, o2_ref, l0_ref, l1_ref, l2_ref, c_ref,
                  wr_ref, wa_ref, wc_ref, wo_ref, y_ref, att_ref, mrg_ref, *, unit_major, tm):
    o_refs = (o0_ref, o1_ref, o2_ref)
    l_refs = (l0_ref, l1_ref, l2_ref)
    i = pl.program_id(0)

    def rows(ref, lead, g):
        pre = () if lead is None else (lead,)
        dil = ATT_GROUPS[g][1]
        if not unit_major or dil == 1:
            return ref[pre + (slice(None), slice(None))]
        nv = tm // 8
        if dil == 4:
            starts = [(j, tm // 8) for j in range(nv)]
        else:
            sb = ATT_GROUPS[g][0]
            q = (i % (sb // tm)) * (tm // 16)
            starts = [(q + ((sb // 2) * (j % 2) + j // 2), ATT_BLK) for j in range(nv)]
        return jnp.concatenate(
            [ref[pre + (pl.ds(st, 8, stride=sd), slice(None))] for st, sd in starts], axis=0)

    lses = [rows(l_refs[g], None, g) for g in range(N_GROUPS)]
    for h in range(H_G):
        lh = [l[:, h:h + 1] for l in lses]
        m = jnp.maximum(jnp.maximum(lh[0], lh[1]), lh[2])
        e = [jnp.exp(v - m) for v in lh]
        den = e[0] + e[1] + e[2]
        hs = slice(h * HD, (h + 1) * HD)
        att = (e[0] / den) * rows(o_refs[0], h, 0)
        att = att + (e[1] / den) * rows(o_refs[1], h, 1)
        att = att + (e[2] / den) * rows(o_refs[2], h, 2)
        att_ref[:, hs] = att.astype(BF16)

    hb = h_ref[...]
    ab = att_ref[...]
    cb = c_ref[...]
    for j in range(D_MODEL // MERGE_CB):
        cs = slice(j * MERGE_CB, (j + 1) * MERGE_CB)
        gs = [slice(br * D_MODEL + j * MERGE_CB, br * D_MODEL + (j + 1) * MERGE_CB)
              for br in range(N_BRANCH)]
        gate = [_sigmoid(g_ref[:, sl].astype(F32)) for sl in gs]
        acc = gate[0] * jnp.dot(hb, wr_ref[:, cs], preferred_element_type=F32)
        acc = acc + gate[1] * jnp.dot(ab, wa_ref[:, cs], preferred_element_type=F32)
        acc = acc + gate[2] * jnp.dot(cb, wc_ref[:, cs], preferred_element_type=F32)
        mrg_ref[:, cs] = acc.astype(BF16)

    y_ref[...] = x_ref[...] + jnp.dot(mrg_ref[...], wo_ref[...], preferred_element_type=F32)


def _merge(x2d, z2d, cg, h2d, o_list, lse_list, c2d, wr, wa, wc, wo, tm, unit_major):
    t, d = x2d.shape
    row = lambda w: pl.BlockSpec((tm, w), lambda i: (i, 0))
    full = lambda a: pl.BlockSpec(a.shape, lambda i: (0, 0))
    ohead = pl.BlockSpec((H_G, tm, HD), lambda i: (0, i, 0))
    o_specs, l_specs = [], []
    for win, dil in ATT_GROUPS:
        if unit_major and dil > 1:
            assert win % tm == 0 and (dil != 4 or win == tm) and tm % 16 == 0
            per = win // tm
            o_specs.append(pl.BlockSpec((H_G, win, HD), lambda i, per=per: (0, i // per, 0)))
            l_specs.append(pl.BlockSpec((win, HD), lambda i, per=per: (i // per, 0)))
        else:
            o_specs.append(ohead)
            l_specs.append(row(HD))
    return pl.pallas_call(
        functools.partial(_merge_kernel, unit_major=unit_major, tm=tm),
        grid=(t // tm,),
        in_specs=[row(d),
                  pl.BlockSpec((tm, N_BRANCH * D_MODEL), lambda i: (i, cg)),
                  row(D_RNN), *o_specs, *l_specs,
                  row(D_CONV), full(wr), full(wa), full(wc), full(wo)],
        out_specs=row(d),
        out_shape=jax.ShapeDtypeStruct((t, d), F32),
        scratch_shapes=[pltpu.VMEM((tm, ATT_W), BF16), pltpu.VMEM((tm, D_MODEL), BF16)],
        compiler_params=_cparams(("parallel",)),
        name="merge",
    )(x2d, z2d, h2d, *o_list, *lse_list, c2d, wr, wa, wc, wo)


def _gelu(x):
    return 0.5 * x * (1.0 + jnp.tanh(0.7978845608028654 * (x + 0.044715 * (x * x * x))))


def _ffn_conv3(up, hist_ref, j, w_ref, b_ref, ext_ref, tm):
    ext_ref[0:8, :] = hist_ref[j]
    ext_ref[8:8 + tm, :] = up
    y = b_ref[...] + w_ref[2:3, :] * up
    y = y + w_ref[1:2, :] * ext_ref[7:7 + tm, :]
    y = y + w_ref[0:1, :] * ext_ref[6:6 + tm, :]
    hist_ref[j] = ext_ref[tm:tm + 8, :]
    return y


def _ffn_prompt_kernel(x_ref, g_ref, wua_ref, wub_ref, wd_ref, cwa_ref, cwb_ref, cba_ref, cbb_ref,
                       y_ref, ta_ref, tb_ref, xn_ref, acc_ref, ha_ref, hb_ref, ea_ref, eb_ref, *, tm):
    i = pl.program_id(1)
    j = pl.program_id(2)
    nj = pl.num_programs(2)

    @pl.when(j == 0)
    def _():
        x = x_ref[...]
        xn_ref[...] = _rmsnorm_rows(x, g_ref[...]).astype(BF16)
        acc_ref[...] = x

    @pl.when(i == 0)
    def _():
        ha_ref[j] = jnp.zeros(ha_ref.shape[1:], F32)
        hb_ref[j] = jnp.zeros(hb_ref.shape[1:], F32)

    xn = xn_ref[...]
    up_a = jnp.dot(xn, wua_ref[...], preferred_element_type=F32)
    fa = _ffn_conv3(up_a, ha_ref, j, cwa_ref, cba_ref, ea_ref, tm)
    up_b = jnp.dot(xn, wub_ref[...], preferred_element_type=F32)
    fb = _ffn_conv3(up_b, hb_ref, j, cwb_ref, cbb_ref, eb_ref, tm)
    ta_ref[...] = ea_ref[tm:tm + 8, :]
    tb_ref[...] = eb_ref[tm:tm + 8, :]
    gact = (_gelu(fa) * fb).astype(BF16)
    acc_ref[...] += jnp.dot(gact, wd_ref[...], preferred_element_type=F32)

    @pl.when(j == nj - 1)
    def _():
        y_ref[...] = acc_ref[...]


def _ffn_prompt(x, g, wu, wd, cw, cb, tm, hc):
    n, s, d = x.shape
    nj = D_FF // hc
    return pl.pallas_call(
        functools.partial(_ffn_prompt_kernel, tm=tm),
        grid=(n, s // tm, nj),
        in_specs=[pl.BlockSpec((None, tm, d), lambda b, i, j: (b, i, 0)),
                  pl.BlockSpec((1, d), lambda b, i, j: (0, 0)),
                  pl.BlockSpec((d, hc), lambda b, i, j: (0, j)),
                  pl.BlockSpec((d, hc), lambda b, i, j: (0, nj + j)),
                  pl.BlockSpec((hc, d), lambda b, i, j: (j, 0)),
                  pl.BlockSpec((FFN_CONV, hc), lambda b, i, j: (0, j)),
                  pl.BlockSpec((FFN_CONV, hc), lambda b, i, j: (0, nj + j)),
                  pl.BlockSpec((1, hc), lambda b, i, j: (0, j)),
                  pl.BlockSpec((1, hc), lambda b, i, j: (0, nj + j))],
        out_specs=[pl.BlockSpec((None, tm, d), lambda b, i, j: (b, i, 0)),
                   pl.BlockSpec((None, None, 8, hc), lambda b, i, j: (b, i, 0, j)),
                   pl.BlockSpec((None, None, 8, hc), lambda b, i, j: (b, i, 0, j))],
        out_shape=[jax.ShapeDtypeStruct((n, s, d), F32),
                   jax.ShapeDtypeStruct((n, s // tm, 8, D_FF), F32),
                   jax.ShapeDtypeStruct((n, s // tm, 8, D_FF), F32)],
        scratch_shapes=[pltpu.VMEM((tm, d), BF16),
                        pltpu.VMEM((tm, d), F32),
                        pltpu.VMEM((nj, 8, hc), F32),
                        pltpu.VMEM((nj, 8, hc), F32),
                        pltpu.VMEM((tm + 8, hc), F32),
                        pltpu.VMEM((tm + 8, hc), F32)],
        compiler_params=_cparams(("parallel", "arbitrary", "arbitrary")),
        name="ffn_prompt",
    )(x, g, wu, wu, wd, cw, cw, cb, cb)


def _state_sample_kernel(z_ref, rh_ref, rbuf_ref, cbuf_ref,
                         rcw_ref, rcb_ref, wa_ref, wx_ref, ba_ref, bx_ref, lam_ref,
                         ccw_ref, ccb_ref, ng_ref, nb_ref,
                         h_ref, c_ref, rnew_ref, cnew_ref):
    x_rnn = z_ref[:, COL_RNN:COL_RNN + D_RNN].astype(F32)
    u = rcb_ref[...] + rcw_ref[RNN_CONV - 1:RNN_CONV, :] * x_rnn
    for k in range(RNN_CONV - 1):
        row = rbuf_ref[k]
        u = u + rcw_ref[k:k + 1, :] * row
        if k > 0:
            rnew_ref[k - 1] = row
    rnew_ref[RNN_CONV - 2] = x_rnn
    ub = u.astype(BF16)
    r_pre = jnp.concatenate(
        [jnp.dot(ub[:, b * RNN_BW:(b + 1) * RNN_BW], wa_ref[b], preferred_element_type=F32)
         for b in range(RNN_BLOCKS)], axis=-1)
    i_pre = jnp.concatenate(
        [jnp.dot(ub[:, b * RNN_BW:(b + 1) * RNN_BW], wx_ref[b], preferred_element_type=F32)
         for b in range(RNN_BLOCKS)], axis=-1)
    a, uu = _lru_coeffs(u, r_pre, i_pre, ba_ref[...], bx_ref[...], _softplus(-lam_ref[...]))
    h_ref[...] = uu + a * rh_ref[...]

    ga = z_ref[:, COL_GLU:COL_GLU + D_CONV].astype(F32)
    gb = z_ref[:, COL_GLU + D_CONV:COL_GLU + 2 * D_CONV].astype(F32)
    c_in = ga * _sigmoid(gb)
    y = ccb_ref[...] + ccw_ref[CCONV - 1:CCONV, :] * c_in
    for k in range(CCONV - 1):
        row = cbuf_ref[k]
        y = y + ccw_ref[k:k + 1, :] * row
        if k > 0:
            cnew_ref[k - 1] = row
    cnew_ref[CCONV - 2] = c_in
    c_ref[...] = _layernorm_swish(y, ng_ref[...], nb_ref[...]).astype(c_ref.dtype)


def _state_sample(z, rh, rbuf, cbuf, rcw, rcb, wa, wx, ba, bx, lam, ccw, ccb, ng, nbias, nb):
    m = z.shape[0]
    wz = COL_GATE
    row = lambda w: pl.BlockSpec((nb, w), lambda i: (i, 0))
    full = lambda a: pl.BlockSpec(a.shape, lambda i: (0,) * a.ndim)
    rspec = pl.BlockSpec((RNN_CONV - 1, nb, D_RNN), lambda i: (0, i, 0))
    cspec = pl.BlockSpec((CCONV - 1, nb, D_CONV), lambda i: (0, i, 0))
    rbuf = rbuf.transpose(1, 0, 2)
    cbuf = cbuf.transpose(1, 0, 2)
    h, c, rnew, cnew = pl.pallas_call(
        _state_sample_kernel,
        grid=(m // nb,),
        in_specs=[row(wz), row(D_RNN), rspec, cspec,
                  full(rcw), full(rcb), full(wa), full(wx), full(ba), full(bx), full(lam),
                  full(ccw), full(ccb), full(ng), full(nbias)],
        out_specs=[row(D_RNN), row(D_CONV), rspec, cspec],
        out_shape=[jax.ShapeDtypeStruct((m, D_RNN), F32),
                   jax.ShapeDtypeStruct((m, D_CONV), BF16),
                   jax.ShapeDtypeStruct((RNN_CONV - 1, m, D_RNN), F32),
                   jax.ShapeDtypeStruct((CCONV - 1, m, D_CONV), F32)],
        compiler_params=_cparams(("parallel",)),
        name="state_sample",
    )(z, rh, rbuf, cbuf, rcw, rcb, wa, wx, ba, bx, lam, ccw, ccb, ng, nbias)
    return h, c, rnew.transpose(1, 0, 2), cnew.transpose(1, 0, 2)


def _attn_sample_kernel(qkv_ref, kv_ref, gq_ref, gk_ref, o_ref, lse_ref, kvn_ref):
    scale = HD ** -0.5
    qn = _rmsnorm_rows(qkv_ref[:, 0], gq_ref[...])
    kn = _rmsnorm_rows(qkv_ref[:, 1], gk_ref[...])
    vn = qkv_ref[:, 2]
    kvn_ref[:, 0] = kn
    kvn_ref[:, 1] = vn
    kc = kv_ref[:, :, 0]
    vc = kv_ref[:, :, 1]
    s_new = jnp.sum(qn * kn, axis=-1, keepdims=True) * scale
    s_buf = jnp.sum(qn[:, None] * kc, axis=-1, keepdims=True) * scale
    m = jnp.maximum(jnp.max(s_buf, axis=1), s_new)
    p_new = jnp.exp(s_new - m)
    p_buf = jnp.exp(s_buf - m[:, None])
    l = p_new + jnp.sum(p_buf, axis=1)
    acc = p_new * vn + jnp.sum(p_buf * vc, axis=1)
    o_ref[...] = acc / l
    lse_ref[...] = jnp.broadcast_to(m + jnp.log(l), lse_ref.shape)


def _attn_sample(qkv, cache, layer, gq, gk, gi, nb):
    m = qkv.shape[0]
    win, dil = ATT_GROUPS[gi]
    depth, mb, L = cache.shape[:3]
    assert L == win and L // dil == ATT_BLK and mb == m
    view = cache.reshape(depth, m, ATT_BLK, dil, 2, H_G, HD)
    hspec = pl.BlockSpec((nb, H_G, HD), lambda i: (i, 0, 0))
    gspec = pl.BlockSpec((1, 1, HD), lambda i: (0, 0, 0))
    return pl.pallas_call(
        _attn_sample_kernel,
        grid=(m // nb,),
        in_specs=[pl.BlockSpec((nb, None, 3, H_G, HD), lambda i: (i, gi, 0, 0, 0)),
                  pl.BlockSpec((None, nb, ATT_BLK, None, 2, H_G, HD),
                               lambda i: (layer, i, 0, 0, 0, 0, 0)),
                  gspec, gspec],
        out_specs=[hspec, hspec, pl.BlockSpec((nb, 2, H_G, HD), lambda i: (i, 0, 0, 0))],
        out_shape=[jax.ShapeDtypeStruct((m, H_G, HD), F32),
                   jax.ShapeDtypeStruct((m, H_G, HD), F32),
                   jax.ShapeDtypeStruct((m, 2, H_G, HD), F32)],
        compiler_params=_cparams(("parallel",)),
        name=f"attn_sample_g{gi}",
    )(qkv, view, gq.reshape(1, 1, HD), gk.reshape(1, 1, HD))


def _ffn_sample_kernel(x_ref, g_ref, buf_ref, wu_ref, wd_ref, cw_ref, cb_ref,
                       y_ref, new_ref, xn_ref, acc_ref):
    j = pl.program_id(0)
    nj = pl.num_programs(0)

    @pl.when(j == 0)
    def _():
        x = x_ref[...]
        xn_ref[...] = _rmsnorm_rows(x, g_ref[...]).astype(BF16)
        acc_ref[...] = x

    f = []
    for half in range(2):
        up = jnp.dot(xn_ref[...], wu_ref[half], preferred_element_type=F32)
        b0 = buf_ref[0, half]
        b1 = buf_ref[1, half]
        y = cb_ref[half] + cw_ref[half, FFN_CONV - 1:FFN_CONV, :] * up
        y = y + cw_ref[half, 0:1, :] * b0 + cw_ref[half, 1:2, :] * b1
        new_ref[0, half] = b1
        new_ref[1, half] = up
        f.append(y)
    gact = (_gelu(f[0]) * f[1]).astype(BF16)
    acc_ref[...] += jnp.dot(gact, wd_ref[...], preferred_element_type=F32)

    @pl.when(j == nj - 1)
    def _():
        y_ref[...] = acc_ref[...]


def _ffn_sample(x, g, buf, wu, wd, cw, cb, hc):
    m, d = x.shape
    nj = D_FF // hc
    buf4 = buf.reshape(m, FFN_CONV - 1, 2, D_FF).transpose(1, 2, 0, 3)
    wu3 = wu.reshape(d, 2, D_FF).transpose(1, 0, 2)
    cw3 = cw.reshape(FFN_CONV, 2, D_FF).transpose(1, 0, 2)
    cb3 = cb.reshape(2, 1, D_FF)
    bspec = pl.BlockSpec((FFN_CONV - 1, 2, m, hc), lambda j: (0, 0, 0, j))
    y, new = pl.pallas_call(
        _ffn_sample_kernel,
        grid=(nj,),
        in_specs=[pl.BlockSpec((m, d), lambda j: (0, 0)),
                  pl.BlockSpec((1, d), lambda j: (0, 0)),
                  bspec,
                  pl.BlockSpec((2, d, hc), lambda j: (0, 0, j)),
                  pl.BlockSpec((hc, d), lambda j: (j, 0)),
                  pl.BlockSpec((2, FFN_CONV, hc), lambda j: (0, 0, j)),
                  pl.BlockSpec((2, 1, hc), lambda j: (0, 0, j))],
        out_specs=[pl.BlockSpec((m, d), lambda j: (0, 0)), bspec],
        out_shape=[jax.ShapeDtypeStruct((m, d), F32),
                   jax.ShapeDtypeStruct((FFN_CONV - 1, 2, m, D_FF), F32)],
        scratch_shapes=[pltpu.VMEM((m, d), BF16), pltpu.VMEM((m, d), F32)],
        compiler_params=_cparams(("arbitrary",)),
        name="ffn_sample",
    )(x, g, buf4, wu3, wd, cw3, cb3)
    return y, new.transpose(2, 0, 1, 3).reshape(m, FFN_CONV - 1, 2 * D_FF)


def _prep_layer_weights(lw):
    w_in = lw['w_in']
    o1 = D_RNN
    o2 = o1 + N_GROUPS * 3 * ATT_W
    o3 = o2 + 2 * D_CONV
    w_perm = jnp.concatenate([w_in[:, o2:o3], w_in[:, :o1], w_in[:, o3:], w_in[:, o1:o2]], axis=1)
    row = lambda v: v.reshape(1, -1)
    return dict(
        norm1_g=row(lw['norm1_g']), w_in=w_perm.astype(BF16),
        rnn_conv_w=lw['rnn_conv_w'], rnn_conv_b=row(lw['rnn_conv_b']),
        rnn_wa=lw['rnn_wa'].astype(BF16), rnn_wx=lw['rnn_wx'].astype(BF16),
        rnn_ba=row(lw['rnn_ba']), rnn_bx=row(lw['rnn_bx']), rnn_lambda=row(lw['rnn_lambda']),
        q_norm_g=lw['q_norm_g'], k_norm_g=lw['k_norm_g'],
        cconv_w=lw['cconv_w'], cconv_b=row(lw['cconv_b']),
        cnorm_g=row(lw['cnorm_g']), cnorm_b=row(lw['cnorm_b']),
        w_br_rnn=lw['w_br_rnn'].astype(BF16), w_br_attn=lw['w_br_attn'].astype(BF16),
        w_br_conv=lw['w_br_conv'].astype(BF16), w_o=lw['w_o'].astype(BF16),
        norm2_g=row(lw['norm2_g']), ffn_up=lw['ffn_up'].astype(BF16),
        ffn_conv_w=lw['ffn_conv_w'], ffn_conv_b=row(lw['ffn_conv_b']),
        ffn_down=lw['ffn_down'].astype(BF16))


def _prompt_layer(x, w, tiles):
    n, s, d = x.shape
    t = n * s
    z_rnn, z_gate, zqkv, c, c_tail = _fused_inproj(
        x, w['norm1_g'], w['w_in'], w['cconv_w'], w['cconv_b'], w['cnorm_g'], w['cnorm_b'],
        tiles['in_tm'])
    zqkv = zqkv.reshape(N_GROUPS * 3, H_G, n, s, HD)
    h_seq, h_last = _rnn_prompt(z_rnn, w['rnn_conv_w'], w['rnn_conv_b'], w['rnn_wa'], w['rnn_wx'],
                                w['rnn_ba'], w['rnn_bx'], w['rnn_lambda'], tiles['rnn_tm'])
    o_list, lse_list, kv_new = [], [], []
    for gi, (win, dil) in enumerate(ATT_GROUPS):
        o, lse, kn = _attn_prompt(zqkv, w['q_norm_g'][gi:gi + 1], w['k_norm_g'][gi:gi + 1], gi,
                                  tiles['att_tt'])
        o_list.append(o.reshape(H_G, t, HD))
        lse_list.append(lse.reshape(t, HD))
        k_tail = kn.transpose(1, 2, 0, 3)
        v_tail = zqkv[3 * gi + 2, :, :, s - win:].transpose(1, 2, 0, 3)
        kv_new.append(jnp.stack([k_tail, v_tail], axis=2))
    x1 = _merge(x.reshape(t, d), z_gate.reshape(t, N_BRANCH * D_MODEL), 0,
                h_seq.reshape(t, D_RNN), o_list, lse_list,
                c.reshape(t, D_CONV), w['w_br_rnn'], w['w_br_attn'], w['w_br_conv'], w['w_o'],
                tiles['mrg_tm'], True)
    x2, ta, tb = _ffn_prompt(x1.reshape(n, s, d), w['norm2_g'], w['ffn_up'], w['ffn_down'],
                             w['ffn_conv_w'], w['ffn_conv_b'], tiles['ffn_tm'], tiles['ffn_hc'])
    rnn_buf_new = z_rnn[:, s - (RNN_CONV - 1):].astype(F32)
    cconv_new = c_tail[:, CC_PAD - (CCONV - 1):]
    ffn_new = jnp.concatenate([ta[:, -1, 8 - (FFN_CONV - 1):], tb[:, -1, 8 - (FFN_CONV - 1):]],
                              axis=-1)
    return x2, kv_new, h_last.reshape(n, D_RNN), rnn_buf_new, cconv_new, ffn_new


def _sample_layer(x, w, layer, rnn_h, rnn_buf, cconv_buf, ffn_buf, caches, tiles):
    m, _, d = x.shape
    x2d = x.reshape(m, d)
    z, zqkv = _inproj(x2d, w['norm1_g'], w['w_in'], m, tiles['in_tn'])
    h, c, rnew, cnew = _state_sample(
        z, rnn_h, rnn_buf, cconv_buf, w['rnn_conv_w'], w['rnn_conv_b'], w['rnn_wa'], w['rnn_wx'],
        w['rnn_ba'], w['rnn_bx'], w['rnn_lambda'], w['cconv_w'], w['cconv_b'],
        w['cnorm_g'], w['cnorm_b'], tiles['smp_state_nb'])
    qkv = zqkv.transpose(1, 0, 2).reshape(m, N_GROUPS, 3, H_G, HD)
    o_list, lse_list, kv_new = [], [], []
    for gi in range(N_GROUPS):
        o, lse, kvn = _attn_sample(qkv, caches[gi], layer, w['q_norm_g'][gi], w['k_norm_g'][gi],
                                   gi, tiles['smp_attn_nb'])
        o_list.append(o.transpose(1, 0, 2))
        lse_list.append(jnp.pad(lse[:, :, 0], ((0, 0), (0, HD - H_G))))
        kv_new.append(kvn.reshape(m, 1, 2, H_G, HD))
    x1 = _merge(x2d, z, COL_GATE // (N_BRANCH * D_MODEL), h.astype(BF16), o_list, lse_list, c,
                w['w_br_rnn'], w['w_br_attn'], w['w_br_conv'], w['w_o'], m, False)
    x2, ffn_new = _ffn_sample(x1, w['norm2_g'], ffn_buf, w['ffn_up'], w['ffn_down'],
                              w['ffn_conv_w'], w['ffn_conv_b'], tiles['ffn_hc'])
    return x2.reshape(m, 1, d), kv_new, h, rnew, cnew, ffn_new


def _tiles(s):
    return dict(in_tm=min(1024, s), in_tn=1536, rnn_tm=min(1024, s), att_tt=2048,
                mrg_tm=min(512, s), ffn_tm=min(1024, s), ffn_hc=1024,
                smp_state_nb=32, smp_attn_nb=8)


def kernel(x_prompt, x_sample, cache_kv_w128, cache_kv_w512, cache_kv_w2048, state_rnn_h, state_rnn_conv, state_cconv, state_ffn_conv, norm1_g, w_in, rnn_conv_w, rnn_conv_b, rnn_wa, rnn_ba, rnn_wx, rnn_bx, rnn_lambda, q_norm_g, k_norm_g, cconv_w, cconv_b, cnorm_g, cnorm_b, w_br_rnn, w_br_attn, w_br_conv, w_o, norm2_g, ffn_up, ffn_conv_w, ffn_conv_b, ffn_down):
    params = dict(norm1_g=norm1_g, w_in=w_in, rnn_conv_w=rnn_conv_w, rnn_conv_b=rnn_conv_b,
                  rnn_wa=rnn_wa, rnn_ba=rnn_ba, rnn_wx=rnn_wx, rnn_bx=rnn_bx,
                  rnn_lambda=rnn_lambda, q_norm_g=q_norm_g, k_norm_g=k_norm_g, cconv_w=cconv_w,
                  cconv_b=cconv_b, cnorm_g=cnorm_g, cnorm_b=cnorm_b, w_br_rnn=w_br_rnn,
                  w_br_attn=w_br_attn, w_br_conv=w_br_conv, w_o=w_o, norm2_g=norm2_g,
                  ffn_up=ffn_up, ffn_conv_w=ffn_conv_w, ffn_conv_b=ffn_conv_b, ffn_down=ffn_down)
    depth = w_in.shape[0]
    assert x_sample.shape[1] == 1
    caches = (cache_kv_w128, cache_kv_w512, cache_kv_w2048)
    tiles = _tiles(x_prompt.shape[1])
    xp, xs = x_prompt, x_sample
    pk, sk = [[], [], []], [[], [], []]
    p_state = [[], [], [], []]
    s_state = [[], [], [], []]
    for l in range(depth):
        w = _prep_layer_weights({k: v[l] for k, v in params.items()})
        xp, kvp, hp, rbp, cbp, fbp = _prompt_layer(xp, w, tiles)
        xs, kvs, hs, rbs, cbs, fbs = _sample_layer(
            xs, w, l, state_rnn_h[l], state_rnn_conv[l], state_cconv[l], state_ffn_conv[l],
            caches, tiles)
        for gi in range(N_GROUPS):
            pk[gi].append(kvp[gi])
            sk[gi].append(kvs[gi])
        for lst, v in zip(p_state, (hp, rbp, cbp, fbp)):
            lst.append(v)
        for lst, v in zip(s_state, (hs, rbs, cbs, fbs)):
            lst.append(v)
    stack = lambda xs_: jnp.stack(xs_, 0)
    return (xp, xs, stack(pk[0]), stack(pk[1]), stack(pk[2]),
            stack(p_state[0]), stack(p_state[1]), stack(p_state[2]), stack(p_state[3]),
            stack(sk[0]), stack(sk[1]), stack(sk[2]),
            stack(s_state[0]), stack(s_state[1]), stack(s_state[2]), stack(s_state[3]))
```

```python
import functools

import jax
import jax.numpy as jnp
from jax import lax
from jax.experimental import pallas as pl
from jax.experimental.pallas import tpu as pltpu

F32 = jnp.float32
BF16 = jnp.bfloat16

D_MODEL = 1024
D_RNN = D_MODEL
RNN_BLOCKS = 8
RNN_BW = D_RNN // RNN_BLOCKS
RNN_CONV = 4
LRU_C = 8.0
ATT_GROUPS = ((128, 1), (512, 4), (2048, 16))
N_GROUPS = len(ATT_GROUPS)
H_G = 4
HD = 128
ATT_W = H_G * HD
ATT_BLK = 128
D_CONV = D_MODEL
CCONV = 31
D_FF = 3 * D_MODEL
FFN_CONV = 3
N_BRANCH = 3
EPS = 1e-6
NEG = -1e30

COL_GLU = 0
COL_RNN = 2 * D_CONV
COL_GATE = COL_RNN + D_RNN
COL_QKV = COL_GATE + N_BRANCH * D_MODEL
N_IN = COL_QKV + N_GROUPS * 3 * ATT_W

VMEM_LIMIT = 56 * 1024 * 1024


def _cparams(sem):
    return pltpu.CompilerParams(dimension_semantics=sem, vmem_limit_bytes=VMEM_LIMIT)


def _sigmoid(x):
    return 0.5 * jnp.tanh(0.5 * x) + 0.5


def _softplus(x):
    return jnp.maximum(x, 0.0) + jnp.log1p(jnp.exp(-jnp.abs(x)))


def _rmsnorm_rows(x, g):
    ms = jnp.mean(x * x, axis=-1, keepdims=True)
    return x * lax.rsqrt(ms + EPS) * g


def _layernorm_swish(y, g, b):
    mu = jnp.mean(y, axis=-1, keepdims=True)
    yc = y - mu
    var = jnp.mean(yc * yc, axis=-1, keepdims=True)
    yn = yc * lax.rsqrt(var + EPS) * g + b
    return yn * _sigmoid(yn)


def _inproj_kernel(x_ref, g_ref, w_ref, o_ref, qkv_ref, xn_ref, *, n_main):
    j = pl.program_id(1)

    @pl.when(j == 0)
    def _():
        xn_ref[...] = _rmsnorm_rows(x_ref[...], g_ref[...]).astype(BF16)

    res = jnp.dot(xn_ref[...], w_ref[...], preferred_element_type=F32)

    @pl.when(j < n_main)
    def _():
        o_ref[...] = res.astype(o_ref.dtype)

    @pl.when(j >= n_main)
    def _():
        for c in range(qkv_ref.shape[0]):
            qkv_ref[c] = res[:, c * HD:(c + 1) * HD]


def _inproj(x2d, g, w, tm, tn):
    t, d = x2d.shape
    n_main = COL_QKV // tn
    n_qkv = (N_IN - COL_QKV) // tn
    hpt = tn // HD
    return pl.pallas_call(
        functools.partial(_inproj_kernel, n_main=n_main),
        grid=(t // tm, n_main + n_qkv),
        in_specs=[pl.BlockSpec((tm, d), lambda i, j: (i, 0)),
                  pl.BlockSpec((1, d), lambda i, j: (0, 0)),
                  pl.BlockSpec((d, tn), lambda i, j: (0, j))],
        out_specs=[pl.BlockSpec((tm, tn), lambda i, j: (i, jnp.minimum(j, n_main - 1))),
                   pl.BlockSpec((hpt, tm, HD), lambda i, j: (jnp.maximum(j - n_main, 0), i, 0))],
        out_shape=[jax.ShapeDtypeStruct((t, COL_QKV), BF16),
                   jax.ShapeDtypeStruct((n_qkv * hpt, t, HD), F32)],
        scratch_shapes=[pltpu.VMEM((tm, d), BF16)],
        compiler_params=_cparams(("parallel", "arbitrary")),
        name="inproj",
    )(x2d, g, w)


def _lru_coeffs(u, r_pre, i_pre, ba, bx, sp):
    r = _sigmoid(r_pre + ba)
    ig = _sigmoid(i_pre + bx)
    log_a = (-LRU_C * r) * sp
    a = jnp.exp(log_a)
    mult = jnp.sqrt(-jnp.tanh(log_a) * (a * a + 1.0))
    return a, mult * (ig * u)


RNN_RC = 256
RNN_NB = 2


def _rnn_prompt_kernel(x_ref, cw_ref, cb_ref, wa_ref, wx_ref, ba_ref, bx_ref, lam_ref,
                       h_ref, hlast_ref, ext_ref, hp_ref, hcar_ref, *, tm):
    i = pl.program_id(1)
    nch = RNN_RC // 64
    rpv = RNN_RC // 8

    @pl.when(i == 0)
    def _():
        ext_ref[:, 0:8, :] = jnp.zeros((RNN_BLOCKS, 8, RNN_BW), F32)
        hcar_ref[...] = jnp.zeros((1, D_RNN), F32)

    for b in range(RNN_BLOCKS):
        ext_ref[b, 8:8 + tm, :] = x_ref[:, b * RNN_BW:(b + 1) * RNN_BW].astype(F32)

    row = lax.broadcasted_iota(jnp.int32, (8, RNN_BW), 0)

    for b0 in range(0, RNN_BLOCKS, RNN_NB):
        bs = tuple(range(b0, b0 + RNN_NB))
        cs = {b: slice(b * RNN_BW, (b + 1) * RNN_BW) for b in bs}
        sp = {b: _softplus(-lam_ref[:, cs[b]]) for b in bs}
        cb = {b: cb_ref[:, cs[b]] for b in bs}
        ba = {b: ba_ref[:, cs[b]] for b in bs}
        bx = {b: bx_ref[:, cs[b]] for b in bs}
        cw = {b: [cw_ref[k:k + 1, cs[b]] for k in range(RNN_CONV)] for b in bs}
        wa = {b: wa_ref[b] for b in bs}
        wx = {b: wx_ref[b] for b in bs}

        def gate_chunk(c, cars, bs=bs, cs=cs, sp=sp, cb=cb, ba=ba, bx=bx, cw=cw, wa=wa, wx=wx):
            r0 = pl.multiple_of(c * RNN_RC, RNN_RC)
            u_all = {}
            for b in bs:
                slabs = [ext_ref[b, pl.ds(r0 + (8 - (RNN_CONV - 1) + wv), rpv, stride=8), :]
                         for wv in range(8 + RNN_CONV - 1)]
                us = []
                for v in range(8):
                    u = cb[b]
                    for k in range(RNN_CONV):
                        u = u + cw[b][k] * slabs[v + k]
                    us.append(u)
                u_all[b] = jnp.concatenate(us, axis=0)
            ub = {b: u_all[b].astype(BF16) for b in bs}
            r_pre = {b: jnp.dot(ub[b], wa[b], preferred_element_type=F32) for b in bs}
            i_pre = {b: jnp.dot(ub[b], wx[b], preferred_element_type=F32) for b in bs}
            coef = {b: _lru_coeffs(u_all[b], r_pre[b], i_pre[b], ba[b], bx[b], sp[b]) for b in bs}
            hs = {b: [] for b in bs}
            ps = {b: [] for b in bs}
            for v in range(8):
                for b in bs:
                    a = coef[b][0][v * rpv:(v + 1) * rpv]
                    uu = coef[b][1][v * rpv:(v + 1) * rpv]
                    hs[b].append(uu if v == 0 else a * hs[b][-1] + uu)
                    ps[b].append(a if v == 0 else a * ps[b][-1])
            car = dict(zip(bs, cars))
            cins = {b: [] for b in bs}
            for q in range(nch):
                for b in bs:
                    pe, he = _sublane_scan(ps[b][-1][8 * q:8 * q + 8], hs[b][-1][8 * q:8 * q + 8], row)
                    e = he + pe * car[b]
                    cins[b].append(jnp.where(row == 0, car[b], pltpu.roll(e, 1, axis=0)))
                    car[b] = e[7:8, :]
            cin = {b: jnp.concatenate(cins[b], axis=0) for b in bs}
            for v in range(8):
                for b in bs:
                    hp_ref[b, v * rpv:(v + 1) * rpv, :] = hs[b][v] + ps[b][v] * cin[b]
            for j in range(0, rpv, 2):
                for b in bs:
                    t0 = hp_ref[b, pl.ds(j, 8, stride=rpv), :]
                    t1 = hp_ref[b, pl.ds(j + 1, 8, stride=rpv), :]
                    dst = pl.multiple_of(r0 + 8 * j, 16)
                    h_ref[pl.ds(dst, 16), cs[b]] = jnp.concatenate([t0, t1], axis=0).astype(h_ref.dtype)
            return tuple(car[b] for b in bs)

        cars = lax.fori_loop(0, tm // RNN_RC, gate_chunk, tuple(hcar_ref[:, cs[b]] for b in bs))
        for b, car in zip(bs, cars):
            hcar_ref[:, cs[b]] = car
            hlast_ref[:, cs[b]] = car
            ext_ref[b, 0:8, :] = ext_ref[b, tm:tm + 8, :]


def _sublane_scan(p, h, row):
    for d in (1, 2, 4):
        hr = pltpu.roll(h, d, axis=0)
        pr = pltpu.roll(p, d, axis=0)
        ok = row >= d
        h = jnp.where(ok, h + p * hr, h)
        p = jnp.where(ok, p * pr, p)
    return p, h


def _rnn_prompt(z, cw, cb, wa, wx, ba, bx, lam, tm):
    n, s, _ = z.shape
    assert tm % RNN_RC == 0
    cblk = 0
    vec = pl.BlockSpec((1, D_RNN), lambda b, i: (0, 0))
    blkw = pl.BlockSpec((RNN_BLOCKS, RNN_BW, RNN_BW), lambda b, i: (0, 0, 0))
    return pl.pallas_call(
        functools.partial(_rnn_prompt_kernel, tm=tm),
        grid=(n, s // tm),
        in_specs=[pl.BlockSpec((None, tm, D_RNN), lambda b, i: (b, i, cblk)),
                  pl.BlockSpec((RNN_CONV, D_RNN), lambda b, i: (0, 0)),
                  vec, blkw, blkw, vec, vec, vec],
        out_specs=[pl.BlockSpec((None, tm, D_RNN), lambda b, i: (b, i, 0)),
                   pl.BlockSpec((None, 1, D_RNN), lambda b, i: (b, 0, 0))],
        out_shape=[jax.ShapeDtypeStruct((n, s, D_RNN), BF16),
                   jax.ShapeDtypeStruct((n, 1, D_RNN), F32)],
        scratch_shapes=[pltpu.VMEM((RNN_BLOCKS, tm + 8, RNN_BW), F32),
                        pltpu.VMEM((RNN_BLOCKS, RNN_RC, RNN_BW), F32),
                        pltpu.VMEM((1, D_RNN), F32)],
        compiler_params=_cparams(("parallel", "arbitrary")),
        name="rnn_prompt",
    )(z, cw, cb, wa, wx, ba, bx, lam)


ATT_UNROLL = 2


def _attn_prompt_kernel(q_ref, k_ref, v_ref, gq_ref, gk_ref,
                        o_ref, lse_ref, kn_ref, ks_ref, vs_ref, knt_ref, *, dil, tt, nblk_total):
    ti = pl.program_id(1)
    nblk = tt // (ATT_BLK * dil)
    units = tt // ATT_BLK
    sb = tt // nblk

    @pl.when(ti == 0)
    def _():
        ks_ref[...] = jnp.zeros(ks_ref.shape, BF16)
        vs_ref[...] = jnp.zeros(vs_ref.shape, BF16)

    sa = lax.broadcasted_iota(jnp.int32, (ATT_BLK, ATT_BLK), 0)
    sk = lax.broadcasted_iota(jnp.int32, (ATT_BLK, ATT_BLK), 1)
    if dil == 4:
        half = ATT_BLK // 2
        pos = lambda a: 2 * (a & (half - 1)) + (a >> (half.bit_length() - 1))
    else:
        pos = lambda a: a
    own_ok = pos(sk) <= pos(sa)
    prev_band = pos(sk) >= pos(sa)
    lane = lax.broadcasted_iota(jnp.int32, (ATT_BLK, HD), 1)
    ones = jnp.ones((ATT_BLK, HD), BF16)
    gq = gq_ref[...] * (HD ** -0.5)
    gk = gk_ref[...]
    dn = (((1,), (1,)), ((), ()))
    upb = units // nblk
    sh = upb.bit_length() - 1

    def load(ref, h, jb, r):
        base = jb * sb + r
        if dil == 1:
            return ref[h, pl.ds(pl.multiple_of(base, ATT_BLK), ATT_BLK), :]
        if dil == 4:
            return jnp.concatenate([ref[h, pl.ds(base, ATT_BLK // 2, stride=8), :],
                                    ref[h, pl.ds(base + 4, ATT_BLK // 2, stride=8), :]], axis=0)
        return ref[h, pl.ds(base, ATT_BLK, stride=dil), :]

    def store_natural(ref, h, r, val):
        if dil == 1:
            ref[h] = val
        elif dil == 4:
            ref[h, pl.ds(r, ATT_BLK // 2, stride=8), :] = val[:ATT_BLK // 2]
            ref[h, pl.ds(r + 4, ATT_BLK // 2, stride=8), :] = val[ATT_BLK // 2:]
        else:
            ref[h, pl.ds(r, ATT_BLK, stride=dil), :] = val

    def unit_group(g, carry):
        us = [g * ATT_UNROLL + e for e in range(ATT_UNROLL)]
        jb = [u >> sh for u in us]
        r = [u & (upb - 1) for u in us]
        gblk = [ti * nblk + j for j in jb]
        par = [gb & 1 for gb in gblk]
        prev_ok = [jnp.logical_and(prev_band, gb > 0) for gb in gblk]
        dst = [_unit_rows(dil, jb[e], sb, r[e]) for e in range(ATT_UNROLL)]
        eh = [(e, h) for e in range(ATT_UNROLL) for h in range(H_G)]
        qb = {k: _rmsnorm_rows(load(q_ref, k[1], jb[k[0]], r[k[0]]), gq).astype(BF16) for k in eh}
        kn = {k: _rmsnorm_rows(load(k_ref, k[1], jb[k[0]], r[k[0]]), gk) for k in eh}
        for e, h in eh:
            knt_ref[e, h] = kn[e, h]
        kb = {k: kn[k].astype(BF16) for k in eh}
        vb = {k: jnp.concatenate([load(v_ref, k[1], jb[k[0]], r[k[0]]).astype(BF16), ones], axis=1)
              for k in eh}
        kprev, vprev = {}, {}
        for e, h in eh:
            if upb == 1 and e > 0:
                kprev[e, h], vprev[e, h] = kb[e - 1, h], vb[e - 1, h]
            else:
                kprev[e, h] = ks_ref[r[e], h, 1 - par[e]]
                vprev[e, h] = vs_ref[r[e], h, 1 - par[e]]
        for e, h in eh:
            ks_ref[r[e], h, par[e]] = kb[e, h]
            vs_ref[r[e], h, par[e]] = vb[e, h]
        s = {k: lax.dot_general(qb[k], jnp.concatenate([kprev[k], kb[k]], axis=0), dn,
                                preferred_element_type=F32) for k in eh}
        s_prev = {k: jnp.where(prev_ok[k[0]], s[k][:, :ATT_BLK], NEG) for k in eh}
        s_own = {k: jnp.where(own_ok, s[k][:, ATT_BLK:], NEG) for k in eh}
        m = {k: jnp.max(jnp.maximum(s_prev[k], s_own[k]), axis=-1, keepdims=True) for k in eh}
        p = {k: jnp.concatenate([jnp.exp(s_prev[k] - m[k]), jnp.exp(s_own[k] - m[k])],
                                axis=1).astype(BF16) for k in eh}
        acc = {k: jnp.dot(p[k], jnp.concatenate([vprev[k], vb[k]], axis=0),
                          preferred_element_type=F32) for k in eh}
        for e in range(ATT_UNROLL):
            lse_tile = jnp.zeros((ATT_BLK, HD), F32)
            for h in range(H_G):
                l = acc[e, h][:, HD:]
                o = acc[e, h][:, :HD] / l
                for lo, n_rows, src in dst[e]:
                    o_ref[h, pl.ds(lo, n_rows), :] = o[src:src + n_rows]
                lse_tile = jnp.where(lane == h, m[e, h] + jnp.log(l), lse_tile)
            for lo, n_rows, src in dst[e]:
                lse_ref[pl.ds(lo, n_rows), :] = lse_tile[src:src + n_rows]

        for e in range(ATT_UNROLL):
            @pl.when(gblk[e] == nblk_total - 1)
            def _(e=e):
                for h in range(H_G):
                    store_natural(kn_ref, h, r[e], knt_ref[e, h])

        return carry

    lax.fori_loop(0, units // ATT_UNROLL, unit_group, 0)


def _unit_rows(dil, jb, sb, r):
    base = jb * sb
    if dil == 1:
        return [(pl.multiple_of(base, ATT_BLK), ATT_BLK, 0)]
    if dil == 4:
        half = ATT_BLK // 2
        return [(pl.multiple_of(base + half * r, half), half, 0),
                (pl.multiple_of(base + sb // 2 + half * r, half), half, half)]
    return [(pl.multiple_of(base + ATT_BLK * r, ATT_BLK), ATT_BLK, 0)]


def _attn_prompt(zqkv, gq, gk, gi, tt):
    _, _, n, s, _ = zqkv.shape
    win, dil = ATT_GROUPS[gi]
    assert win == ATT_BLK * dil and tt % win == 0 and s % tt == 0
    upb = tt // ATT_BLK // (tt // win)
    spec = lambda c: pl.BlockSpec((None, H_G, None, tt, HD), lambda b, i, c=c: (c, 0, b, i, 0))
    gspec = pl.BlockSpec((1, HD), lambda b, i: (0, 0))
    hspec = pl.BlockSpec((H_G, None, tt, HD), lambda b, i: (0, b, i, 0))
    return pl.pallas_call(
        functools.partial(_attn_prompt_kernel, dil=dil, tt=tt, nblk_total=s // win),
        grid=(n, s // tt),
        in_specs=[spec(3 * gi), spec(3 * gi + 1), spec(3 * gi + 2), gspec, gspec],
        out_specs=[hspec, pl.BlockSpec((None, tt, HD), lambda b, i: (b, i, 0)),
                   pl.BlockSpec((H_G, None, win, HD), lambda b, i: (0, b, 0, 0))],
        out_shape=[jax.ShapeDtypeStruct((H_G, n, s, HD), F32),
                   jax.ShapeDtypeStruct((n, s, HD), F32),
                   jax.ShapeDtypeStruct((H_G, n, win, HD), F32)],
        scratch_shapes=[pltpu.VMEM((upb, H_G, 2, ATT_BLK, HD), BF16),
                        pltpu.VMEM((upb, H_G, 2, ATT_BLK, 2 * HD), BF16),
                        pltpu.VMEM((ATT_UNROLL, H_G, ATT_BLK, HD), F32)],
        compiler_params=_cparams(("parallel", "arbitrary")),
        name=f"attn_prompt_g{gi}",
    )(zqkv, zqkv, zqkv, gq, gk)


CC_PAD = 32
CC_RC = 64
CC_LB = 128


FIN_TN = 1536
FIN_PW = 256
FIN_CONV_STEPS = ((0, 3), (3, 6), (6, 9), (9, 12), (12, 16))


def _fused_inproj_kernel(x_ref, g_ref, w_ref, ccw_ref, ccb_ref, ng_ref, nb_ref,
                         rnn_ref, gate_ref, qkv_ref, c_ref, tail_ref,
                         xn_ref, ga_ref, ext_ref, y_ref, *, tm):
    i = pl.program_id(1)
    j = pl.program_id(2)
    npc = FIN_TN // FIN_PW
    nlb = D_CONV // CC_LB
    nv = CC_RC // 8
    off = CC_PAD - (CCONV - 1)

    def piece(p):
        return jnp.dot(xn_ref[...], w_ref[:, p * FIN_PW:(p + 1) * FIN_PW],
                       preferred_element_type=F32)

    def run(sink, units):
        per = -(-len(units) // npc)
        for p in range(npc):
            sink(p, piece(p))
            for unit in units[p * per:(p + 1) * per]:
                unit()

    def glu_to_ext(lb, gb):
        ls = slice(lb * CC_LB, (lb + 1) * CC_LB)
        ext_ref[lb, CC_PAD:CC_PAD + tm, :] = ga_ref[:, ls] * _sigmoid(gb)

    def conv_unit(q, lb):
        ls = slice(lb * CC_LB, (lb + 1) * CC_LB)
        r0 = q * CC_RC
        slabs = {}

        def slab(wv):
            if wv not in slabs:
                slabs[wv] = ext_ref[lb, pl.ds(r0 + off + wv, 8, stride=nv), :]
            return slabs[wv]

        acc = [jnp.zeros((8, CC_LB), F32) + ccb_ref[:, ls]] * nv
        for k in range(CCONV):
            wk = ccw_ref[k:k + 1, ls]
            for v in range(nv):
                acc[v] = acc[v] + wk * slab(v + k)
        y_ref[q % 2, lb] = jnp.concatenate(acc, axis=0)

    def norm_unit(q):
        y = jnp.concatenate(
            [jnp.concatenate([y_ref[q % 2, lb, pl.ds(t, 8, stride=nv), :] for t in range(nv)], axis=0)
             for lb in range(nlb)], axis=-1)
        c_ref[q * CC_RC:(q + 1) * CC_RC, :] = _layernorm_swish(
            y, ng_ref[...], nb_ref[...]).astype(c_ref.dtype)

    def conv_units(q_lo, q_hi):
        units = []
        for q in range(q_lo, q_hi):
            units += [functools.partial(conv_unit, q, lb) for lb in range(nlb)]
            units.append(functools.partial(norm_unit, q))
        return units

    @pl.when(j == 0)
    def _():
        xn_ref[...] = _rmsnorm_rows(x_ref[...], g_ref[...]).astype(BF16)

        @pl.when(i == 0)
        def _():
            ext_ref[:, 0:CC_PAD, :] = jnp.zeros((nlb, CC_PAD, CC_LB), F32)

        def sink(p, val):
            if p * FIN_PW < D_CONV:
                ga_ref[:, p * FIN_PW:(p + 1) * FIN_PW] = val
            else:
                for h in range(FIN_PW // CC_LB):
                    lb = (p * FIN_PW - D_CONV) // CC_LB + h
                    glu_to_ext(lb, val[:, h * CC_LB:(h + 1) * CC_LB])

        run(sink, [])

    @pl.when(j == 1)
    def _():
        n_gb = (2 * D_CONV - FIN_TN) // FIN_PW

        def sink(p, val):
            if p < n_gb:
                for h in range(FIN_PW // CC_LB):
                    lb = (FIN_TN - D_CONV + p * FIN_PW) // CC_LB + h
                    glu_to_ext(lb, val[:, h * CC_LB:(h + 1) * CC_LB])
            else:
                c0 = (p - n_gb) * FIN_PW
                rnn_ref[:, c0:c0 + FIN_PW] = val.astype(rnn_ref.dtype)

        run(sink, [])
        for lb in range(nlb):
            tail_ref[:, lb * CC_LB:(lb + 1) * CC_LB] = ext_ref[lb, tm:tm + CC_PAD, :]

    def gate_sink(p, val):
        gate_ref[:, p * FIN_PW:(p + 1) * FIN_PW] = val.astype(gate_ref.dtype)

    def qkv_sink(p, val):
        for h in range(FIN_PW // HD):
            qkv_ref[p * (FIN_PW // HD) + h] = val[:, h * HD:(h + 1) * HD]

    for step, (q_lo, q_hi) in enumerate(FIN_CONV_STEPS):
        @pl.when(j == 2 + step)
        def _(step=step, q_lo=q_lo, q_hi=q_hi):
            run(gate_sink if step < 2 else qkv_sink, conv_units(q_lo, q_hi))

    @pl.when(j == pl.num_programs(2) - 1)
    def _():
        for lb in range(nlb):
            ext_ref[lb, 0:CC_PAD, :] = ext_ref[lb, tm:tm + CC_PAD, :]


def _fused_inproj(x, g, w, ccw, ccb, ng, nb, tm):
    n, s, d = x.shape
    nj = N_IN // FIN_TN
    assert (COL_RNN + D_RNN, COL_GATE, COL_QKV) == (2 * FIN_TN, 2 * FIN_TN, 4 * FIN_TN)
    assert nj == 2 + len(FIN_CONV_STEPS) and tm == FIN_CONV_STEPS[-1][1] * CC_RC
    hpt = FIN_TN // HD
    vec = pl.BlockSpec((1, D_CONV), lambda b, i, j: (0, 0))
    return pl.pallas_call(
        functools.partial(_fused_inproj_kernel, tm=tm),
        grid=(n, s // tm, nj),
        in_specs=[pl.BlockSpec((None, tm, d), lambda b, i, j: (b, i, 0)),
                  pl.BlockSpec((1, d), lambda b, i, j: (0, 0)),
                  pl.BlockSpec((d, FIN_TN), lambda b, i, j: (0, j)),
                  pl.BlockSpec((CCONV, D_CONV), lambda b, i, j: (0, 0)),
                  vec, vec, vec],
        out_specs=[pl.BlockSpec((None, tm, D_RNN), lambda b, i, j: (b, i, 0)),
                   pl.BlockSpec((None, tm, FIN_TN), lambda b, i, j: (b, i, jnp.clip(j - 2, 0, 1))),
                   pl.BlockSpec((hpt, None, tm, HD), lambda b, i, j: (jnp.clip(j - 4, 0, 2), b, i, 0)),
                   pl.BlockSpec((None, tm, D_CONV), lambda b, i, j: (b, i, 0)),
                   pl.BlockSpec((None, CC_PAD, D_CONV), lambda b, i, j: (b, 0, 0))],
        out_shape=[jax.ShapeDtypeStruct((n, s, D_RNN), BF16),
                   jax.ShapeDtypeStruct((n, s, N_BRANCH * D_MODEL), BF16),
                   jax.ShapeDtypeStruct((N_GROUPS * 3 * H_G, n, s, HD), F32),
                   jax.ShapeDtypeStruct((n, s, D_CONV), BF16),
                   jax.ShapeDtypeStruct((n, CC_PAD, D_CONV), F32)],
        scratch_shapes=[pltpu.VMEM((tm, d), BF16),
                        pltpu.VMEM((tm, D_CONV), F32),
                        pltpu.VMEM((D_CONV // CC_LB, tm + CC_PAD, CC_LB), F32),
                        pltpu.VMEM((2, D_CONV // CC_LB, CC_RC, CC_LB), F32)],
        compiler_params=_cparams(("parallel", "arbitrary", "arbitrary")),
        name="inproj_cconv",
    )(x, g, w, ccw, ccb, ng, nb)


MERGE_CB = 512


def _merge_kernel(x_ref, g_ref, h_ref, o0_ref, o1_ref, o2_ref, l0_ref, l1_ref, l2_ref, c_ref,
                  wr_ref, wa_ref, wc_ref, wo_ref, y_ref, att_ref, mrg_ref, *, unit_major, tm):
    o_refs = (o0_ref, o1_ref, o2_ref)
    l_refs = (l0_ref, l1_ref, l2_ref)
    i = pl.program_id(0)

    def rows(ref, lead, g):
        pre = () if lead is None else (lead,)
        dil = ATT_GROUPS[g][1]
        if not unit_major or dil == 1:
            return ref[pre + (slice(None), slice(None))]
        nv = tm // 8
        if dil == 4:
            starts = [(j, tm // 8) for j in range(nv)]
        else:
            sb = ATT_GROUPS[g][0]
            q = (i % (sb // tm)) * (tm // 16)
            starts = [(q + ((sb // 2) * (j % 2) + j // 2), ATT_BLK) for j in range(nv)]
        return jnp.concatenate(
            [ref[pre + (pl.ds(st, 8, stride=sd), slice(None))] for st, sd in starts], axis=0)

    lses = [rows(l_refs[g], None, g) for g in range(N_GROUPS)]
    for h in range(H_G):
        lh = [l[:, h:h + 1] for l in lses]
        m = jnp.maximum(jnp.maximum(lh[0], lh[1]), lh[2])
        e = [jnp.exp(v - m) for v in lh]
        den = e[0] + e[1] + e[2]
        hs = slice(h * HD, (h + 1) * HD)
        att = (e[0] / den) * rows(o_refs[0], h, 0)
        att = att + (e[1] / den) * rows(o_refs[1], h, 1)
        att = att + (e[2] / den) * rows(o_refs[2], h, 2)
        att_ref[:, hs] = att.astype(BF16)

    hb = h_ref[...]
    ab = att_ref[...]
    cb = c_ref[...]
    for j in range(D_MODEL // MERGE_CB):
        cs = slice(j * MERGE_CB, (j + 1) * MERGE_CB)
        gs = [slice(br * D_MODEL + j * MERGE_CB, br * D_MODEL + (j + 1) * MERGE_CB)
              for br in range(N_BRANCH)]
        gate = [_sigmoid(g_ref[:, sl].astype(F32)) for sl in gs]
        acc = gate[0] * jnp.dot(hb, wr_ref[:, cs], preferred_element_type=F32)
        acc = acc + gate[1] * jnp.dot(ab, wa_ref[:, cs], preferred_element_type=F32)
        acc = acc + gate[2] * jnp.dot(cb, wc_ref[:, cs], preferred_element_type=F32)
        mrg_ref[:, cs] = acc.astype(BF16)

    y_ref[...] = x_ref[...] + jnp.dot(mrg_ref[...], wo_ref[...], preferred_element_type=F32)


def _merge(x2d, z2d, cg, h2d, o_list, lse_list, c2d, wr, wa, wc, wo, tm, unit_major):
    t, d = x2d.shape
    row = lambda w: pl.BlockSpec((tm, w), lambda i: (i, 0))
    full = lambda a: pl.BlockSpec(a.shape, lambda i: (0, 0))
    ohead = pl.BlockSpec((H_G, tm, HD), lambda i: (0, i, 0))
    o_specs, l_specs = [], []
    for win, dil in ATT_GROUPS:
        if unit_major and dil > 1:
            assert win % tm == 0 and (dil != 4 or win == tm) and tm % 16 == 0
            per = win // tm
            o_specs.append(pl.BlockSpec((H_G, win, HD), lambda i, per=per: (0, i // per, 0)))
            l_specs.append(pl.BlockSpec((win, HD), lambda i, per=per: (i // per, 0)))
        else:
            o_specs.append(ohead)
            l_specs.append(row(HD))
    return pl.pallas_call(
        functools.partial(_merge_kernel, unit_major=unit_major, tm=tm),
        grid=(t // tm,),
        in_specs=[row(d),
                  pl.BlockSpec((tm, N_BRANCH * D_MODEL), lambda i: (i, cg)),
                  row(D_RNN), *o_specs, *l_specs,
                  row(D_CONV), full(wr), full(wa), full(wc), full(wo)],
        out_specs=row(d),
        out_shape=jax.ShapeDtypeStruct((t, d), F32),
        scratch_shapes=[pltpu.VMEM((tm, ATT_W), BF16), pltpu.VMEM((tm, D_MODEL), BF16)],
        compiler_params=_cparams(("parallel",)),
        name="merge",
    )(x2d, z2d, h2d, *o_list, *lse_list, c2d, wr, wa, wc, wo)


def _gelu(x):
    return 0.5 * x * (1.0 + jnp.tanh(0.7978845608028654 * (x + 0.044715 * (x * x * x))))


def _ffn_conv3(up, hist_ref, j, w_ref, b_ref, ext_ref, tm):
    ext_ref[0:8, :] = hist_ref[j]
    ext_ref[8:8 + tm, :] = up
    y = b_ref[...] + w_ref[2:3, :] * up
    y = y + w_ref[1:2, :] * ext_ref[7:7 + tm, :]
    y = y + w_ref[0:1, :] * ext_ref[6:6 + tm, :]
    hist_ref[j] = ext_ref[tm:tm + 8, :]
    return y


def _ffn_prompt_kernel(x_ref, g_ref, wua_ref, wub_ref, wd_ref, cwa_ref, cwb_ref, cba_ref, cbb_ref,
                       y_ref, ta_ref, tb_ref, xn_ref, acc_ref, ha_ref, hb_ref, ea_ref, eb_ref, *, tm):
    i = pl.program_id(1)
    j = pl.program_id(2)
    nj = pl.num_programs(2)

    @pl.when(j == 0)
    def _():
        x = x_ref[...]
        xn_ref[...] = _rmsnorm_rows(x, g_ref[...]).astype(BF16)
        acc_ref[...] = x

    @pl.when(i == 0)
    def _():
        ha_ref[j] = jnp.zeros(ha_ref.shape[1:], F32)
        hb_ref[j] = jnp.zeros(hb_ref.shape[1:], F32)

    xn = xn_ref[...]
    up_a = jnp.dot(xn, wua_ref[...], preferred_element_type=F32)
    fa = _ffn_conv3(up_a, ha_ref, j, cwa_ref, cba_ref, ea_ref, tm)
    up_b = jnp.dot(xn, wub_ref[...], preferred_element_type=F32)
    fb = _ffn_conv3(up_b, hb_ref, j, cwb_ref, cbb_ref, eb_ref, tm)
    ta_ref[...] = ea_ref[tm:tm + 8, :]
    tb_ref[...] = eb_ref[tm:tm + 8, :]
    gact = (_gelu(fa) * fb).astype(BF16)
    acc_ref[...] += jnp.dot(gact, wd_ref[...], preferred_element_type=F32)

    @pl.when(j == nj - 1)
    def _():
        y_ref[...] = acc_ref[...]


def _ffn_prompt(x, g, wu, wd, cw, cb, tm, hc):
    n, s, d = x.shape
    nj = D_FF // hc
    return pl.pallas_call(
        functools.partial(_ffn_prompt_kernel, tm=tm),
        grid=(n, s // tm, nj),
        in_specs=[pl.BlockSpec((None, tm, d), lambda b, i, j: (b, i, 0)),
                  pl.BlockSpec((1, d), lambda b, i, j: (0, 0)),
                  pl.BlockSpec((d, hc), lambda b, i, j: (0, j)),
                  pl.BlockSpec((d, hc), lambda b, i, j: (0, nj + j)),
                  pl.BlockSpec((hc, d), lambda b, i, j: (j, 0)),
                  pl.BlockSpec((FFN_CONV, hc), lambda b, i, j: (0, j)),
                  pl.BlockSpec((FFN_CONV, hc), lambda b, i, j: (0, nj + j)),
                  pl.BlockSpec((1, hc), lambda b, i, j: (0, j)),
                  pl.BlockSpec((1, hc), lambda b, i, j: (0, nj + j))],
        out_specs=[pl.BlockSpec((None, tm, d), lambda b, i, j: (b, i, 0)),
                   pl.BlockSpec((None, None, 8, hc), lambda b, i, j: (b, i, 0, j)),
                   pl.BlockSpec((None, None, 8, hc), lambda b, i, j: (b, i, 0, j))],
        out_shape=[jax.ShapeDtypeStruct((n, s, d), F32),
                   jax.ShapeDtypeStruct((n, s // tm, 8, D_FF), F32),
                   jax.ShapeDtypeStruct((n, s // tm, 8, D_FF), F32)],
        scratch_shapes=[pltpu.VMEM((tm, d), BF16),
                        pltpu.VMEM((tm, d), F32),
                        pltpu.VMEM((nj, 8, hc), F32),
                        pltpu.VMEM((nj, 8, hc), F32),
                        pltpu.VMEM((tm + 8, hc), F32),
                        pltpu.VMEM((tm + 8, hc), F32)],
        compiler_params=_cparams(("parallel", "arbitrary", "arbitrary")),
        name="ffn_prompt",
    )(x, g, wu, wu, wd, cw, cw, cb, cb)


def _state_sample_kernel(z_ref, rh_ref, rbuf_ref, cbuf_ref,
                         rcw_ref, rcb_ref, wa_ref, wx_ref, ba_ref, bx_ref, lam_ref,
                         ccw_ref, ccb_ref, ng_ref, nb_ref,
                         h_ref, c_ref, rnew_ref, cnew_ref):
    x_rnn = z_ref[:, COL_RNN:COL_RNN + D_RNN].astype(F32)
    u = rcb_ref[...] + rcw_ref[RNN_CONV - 1:RNN_CONV, :] * x_rnn
    for k in range(RNN_CONV - 1):
        row = rbuf_ref[k]
        u = u + rcw_ref[k:k + 1, :] * row
        if k > 0:
            rnew_ref[k - 1] = row
    rnew_ref[RNN_CONV - 2] = x_rnn
    ub = u.astype(BF16)
    r_pre = jnp.concatenate(
        [jnp.dot(ub[:, b * RNN_BW:(b + 1) * RNN_BW], wa_ref[b], preferred_element_type=F32)
         for b in range(RNN_BLOCKS)], axis=-1)
    i_pre = jnp.concatenate(
        [jnp.dot(ub[:, b * RNN_BW:(b + 1) * RNN_BW], wx_ref[b], preferred_element_type=F32)
         for b in range(RNN_BLOCKS)], axis=-1)
    a, uu = _lru_coeffs(u, r_pre, i_pre, ba_ref[...], bx_ref[...], _softplus(-lam_ref[...]))
    h_ref[...] = uu + a * rh_ref[...]

    ga = z_ref[:, COL_GLU:COL_GLU + D_CONV].astype(F32)
    gb = z_ref[:, COL_GLU + D_CONV:COL_GLU + 2 * D_CONV].astype(F32)
    c_in = ga * _sigmoid(gb)
    y = ccb_ref[...] + ccw_ref[CCONV - 1:CCONV, :] * c_in
    for k in range(CCONV - 1):
        row = cbuf_ref[k]
        y = y + ccw_ref[k:k + 1, :] * row
        if k > 0:
            cnew_ref[k - 1] = row
    cnew_ref[CCONV - 2] = c_in
    c_ref[...] = _layernorm_swish(y, ng_ref[...], nb_ref[...]).astype(c_ref.dtype)


def _state_sample(z, rh, rbuf, cbuf, rcw, rcb, wa, wx, ba, bx, lam, ccw, ccb, ng, nbias, nb):
    m = z.shape[0]
    wz = COL_GATE
    row = lambda w: pl.BlockSpec((nb, w), lambda i: (i, 0))
    full = lambda a: pl.BlockSpec(a.shape, lambda i: (0,) * a.ndim)
    rspec = pl.BlockSpec((RNN_CONV - 1, nb, D_RNN), lambda i: (0, i, 0))
    cspec = pl.BlockSpec((CCONV - 1, nb, D_CONV), lambda i: (0, i, 0))
    rbuf = rbuf.transpose(1, 0, 2)
    cbuf = cbuf.transpose(1, 0, 2)
    h, c, rnew, cnew = pl.pallas_call(
        _state_sample_kernel,
        grid=(m // nb,),
        in_specs=[row(wz), row(D_RNN), rspec, cspec,
                  full(rcw), full(rcb), full(wa), full(wx), full(ba), full(bx), full(lam),
                  full(ccw), full(ccb), full(ng), full(nbias)],
        out_specs=[row(D_RNN), row(D_CONV), rspec, cspec],
        out_shape=[jax.ShapeDtypeStruct((m, D_RNN), F32),
                   jax.ShapeDtypeStruct((m, D_CONV), BF16),
                   jax.ShapeDtypeStruct((RNN_CONV - 1, m, D_RNN), F32),
                   jax.ShapeDtypeStruct((CCONV - 1, m, D_CONV), F32)],
        compiler_params=_cparams(("parallel",)),
        name="state_sample",
    )(z, rh, rbuf, cbuf, rcw, rcb, wa, wx, ba, bx, lam, ccw, ccb, ng, nbias)
    return h, c, rnew.transpose(1, 0, 2), cnew.transpose(1, 0, 2)


def _attn_sample_kernel(qkv_ref, kv_ref, gq_ref, gk_ref, o_ref, lse_ref, kvn_ref):
    log2e = 1.4426950408889634
    qn = _rmsnorm_rows(qkv_ref[:, 0], gq_ref[...] * (HD ** -0.5 * log2e))
    kn = _rmsnorm_rows(qkv_ref[:, 1], gk_ref[...])
    vn = qkv_ref[:, 2]
    kvn_ref[:, 0] = kn
    kvn_ref[:, 1] = vn
    kc = kv_ref[:, :, 0]
    vc = kv_ref[:, :, 1]
    s_new = jnp.sum(qn * kn, axis=-1, keepdims=True)
    s_buf = jnp.sum(qn[:, None] * kc, axis=-1, keepdims=True)
    m = jnp.maximum(jnp.max(s_buf, axis=1), s_new)
    p_new = jnp.exp2(s_new - m)
    p_buf = jnp.exp2(s_buf - m[:, None])
    l = p_new + jnp.sum(p_buf, axis=1)
    acc = p_new * vn + jnp.sum(p_buf * vc, axis=1)
    o_ref[...] = acc / l
    lse_ref[...] = jnp.broadcast_to(m * (1.0 / log2e) + jnp.log(l), lse_ref.shape)


def _attn_sample(qkv, cache, layer, gq, gk, gi, nb):
    m = qkv.shape[0]
    win, dil = ATT_GROUPS[gi]
    depth, mb, L = cache.shape[:3]
    assert L == win and L // dil == ATT_BLK and mb == m
    view = cache.reshape(depth, m, ATT_BLK, dil, 2, H_G, HD)
    hspec = pl.BlockSpec((nb, H_G, HD), lambda i: (i, 0, 0))
    gspec = pl.BlockSpec((1, 1, HD), lambda i: (0, 0, 0))
    return pl.pallas_call(
        _attn_sample_kernel,
        grid=(m // nb,),
        in_specs=[pl.BlockSpec((nb, None, 3, H_G, HD), lambda i: (i, gi, 0, 0, 0)),
                  pl.BlockSpec((None, nb, ATT_BLK, None, 2, H_G, HD),
                               lambda i: (layer, i, 0, 0, 0, 0, 0)),
                  gspec, gspec],
        out_specs=[hspec, hspec, pl.BlockSpec((nb, 2, H_G, HD), lambda i: (i, 0, 0, 0))],
        out_shape=[jax.ShapeDtypeStruct((m, H_G, HD), F32),
                   jax.ShapeDtypeStruct((m, H_G, HD), F32),
                   jax.ShapeDtypeStruct((m, 2, H_G, HD), F32)],
        compiler_params=_cparams(("parallel",)),
        name=f"attn_sample_g{gi}",
    )(qkv, view, gq.reshape(1, 1, HD), gk.reshape(1, 1, HD))


def _ffn_sample_kernel(x_ref, g_ref, buf_ref, wu_ref, wd_ref, cw_ref, cb_ref,
                       y_ref, new_ref, xn_ref, acc_ref):
    j = pl.program_id(0)
    nj = pl.num_programs(0)

    @pl.when(j == 0)
    def _():
        x = x_ref[...]
        xn_ref[...] = _rmsnorm_rows(x, g_ref[...]).astype(BF16)
        acc_ref[...] = x

    f = []
    for half in range(2):
        up = jnp.dot(xn_ref[...], wu_ref[half], preferred_element_type=F32)
        b0 = buf_ref[0, half]
        b1 = buf_ref[1, half]
        y = cb_ref[half] + cw_ref[half, FFN_CONV - 1:FFN_CONV, :] * up
        y = y + cw_ref[half, 0:1, :] * b0 + cw_ref[half, 1:2, :] * b1
        new_ref[0, half] = b1
        new_ref[1, half] = up
        f.append(y)
    gact = (_gelu(f[0]) * f[1]).astype(BF16)
    acc_ref[...] += jnp.dot(gact, wd_ref[...], preferred_element_type=F32)

    @pl.when(j == nj - 1)
    def _():
        y_ref[...] = acc_ref[...]


def _ffn_sample(x, g, buf, wu, wd, cw, cb, hc):
    m, d = x.shape
    nj = D_FF // hc
    buf4 = buf.reshape(m, FFN_CONV - 1, 2, D_FF).transpose(1, 2, 0, 3)
    wu3 = wu.reshape(d, 2, D_FF).transpose(1, 0, 2)
    cw3 = cw.reshape(FFN_CONV, 2, D_FF).transpose(1, 0, 2)
    cb3 = cb.reshape(2, 1, D_FF)
    bspec = pl.BlockSpec((FFN_CONV - 1, 2, m, hc), lambda j: (0, 0, 0, j))
    y, new = pl.pallas_call(
        _ffn_sample_kernel,
        grid=(nj,),
        in_specs=[pl.BlockSpec((m, d), lambda j: (0, 0)),
                  pl.BlockSpec((1, d), lambda j: (0, 0)),
                  bspec,
                  pl.BlockSpec((2, d, hc), lambda j: (0, 0, j)),
                  pl.BlockSpec((hc, d), lambda j: (j, 0)),
                  pl.BlockSpec((2, FFN_CONV, hc), lambda j: (0, 0, j)),
                  pl.BlockSpec((2, 1, hc), lambda j: (0, 0, j))],
        out_specs=[pl.BlockSpec((m, d), lambda j: (0, 0)), bspec],
        out_shape=[jax.ShapeDtypeStruct((m, d), F32),
                   jax.ShapeDtypeStruct((FFN_CONV - 1, 2, m, D_FF), F32)],
        scratch_shapes=[pltpu.VMEM((m, d), BF16), pltpu.VMEM((m, d), F32)],
        compiler_params=_cparams(("arbitrary",)),
        name="ffn_sample",
    )(x, g, buf4, wu3, wd, cw3, cb3)
    return y, new.transpose(2, 0, 1, 3).reshape(m, FFN_CONV - 1, 2 * D_FF)


def _prep_layer_weights(lw):
    w_in = lw['w_in']
    o1 = D_RNN
    o2 = o1 + N_GROUPS * 3 * ATT_W
    o3 = o2 + 2 * D_CONV
    w_perm = jnp.concatenate([w_in[:, o2:o3], w_in[:, :o1], w_in[:, o3:], w_in[:, o1:o2]], axis=1)
    row = lambda v: v.reshape(1, -1)
    return dict(
        norm1_g=row(lw['norm1_g']), w_in=w_perm.astype(BF16),
        rnn_conv_w=lw['rnn_conv_w'], rnn_conv_b=row(lw['rnn_conv_b']),
        rnn_wa=lw['rnn_wa'].astype(BF16), rnn_wx=lw['rnn_wx'].astype(BF16),
        rnn_ba=row(lw['rnn_ba']), rnn_bx=row(lw['rnn_bx']), rnn_lambda=row(lw['rnn_lambda']),
        q_norm_g=lw['q_norm_g'], k_norm_g=lw['k_norm_g'],
        cconv_w=lw['cconv_w'], cconv_b=row(lw['cconv_b']),
        cnorm_g=row(lw['cnorm_g']), cnorm_b=row(lw['cnorm_b']),
        w_br_rnn=lw['w_br_rnn'].astype(BF16), w_br_attn=lw['w_br_attn'].astype(BF16),
        w_br_conv=lw['w_br_conv'].astype(BF16), w_o=lw['w_o'].astype(BF16),
        norm2_g=row(lw['norm2_g']), ffn_up=lw['ffn_up'].astype(BF16),
        ffn_conv_w=lw['ffn_conv_w'], ffn_conv_b=row(lw['ffn_conv_b']),
        ffn_down=lw['ffn_down'].astype(BF16))


def _prompt_layer(x, w, tiles):
    n, s, d = x.shape
    t = n * s
    z_rnn, z_gate, zqkv, c, c_tail = _fused_inproj(
        x, w['norm1_g'], w['w_in'], w['cconv_w'], w['cconv_b'], w['cnorm_g'], w['cnorm_b'],
        tiles['in_tm'])
    zqkv = zqkv.reshape(N_GROUPS * 3, H_G, n, s, HD)
    h_seq, h_last = _rnn_prompt(z_rnn, w['rnn_conv_w'], w['rnn_conv_b'], w['rnn_wa'], w['rnn_wx'],
                                w['rnn_ba'], w['rnn_bx'], w['rnn_lambda'], tiles['rnn_tm'])
    o_list, lse_list, kv_new = [], [], []
    for gi, (win, dil) in enumerate(ATT_GROUPS):
        o, lse, kn = _attn_prompt(zqkv, w['q_norm_g'][gi:gi + 1], w['k_norm_g'][gi:gi + 1], gi,
                                  tiles['att_tt'])
        o_list.append(o.reshape(H_G, t, HD))
        lse_list.append(lse.reshape(t, HD))
        k_tail = kn.transpose(1, 2, 0, 3)
        v_tail = zqkv[3 * gi + 2, :, :, s - win:].transpose(1, 2, 0, 3)
        kv_new.append(jnp.stack([k_tail, v_tail], axis=2))
    x1 = _merge(x.reshape(t, d), z_gate.reshape(t, N_BRANCH * D_MODEL), 0,
                h_seq.reshape(t, D_RNN), o_list, lse_list,
                c.reshape(t, D_CONV), w['w_br_rnn'], w['w_br_attn'], w['w_br_conv'], w['w_o'],
                tiles['mrg_tm'], True)
    x2, ta, tb = _ffn_prompt(x1.reshape(n, s, d), w['norm2_g'], w['ffn_up'], w['ffn_down'],
                             w['ffn_conv_w'], w['ffn_conv_b'], tiles['ffn_tm'], tiles['ffn_hc'])
    rnn_buf_new = z_rnn[:, s - (RNN_CONV - 1):].astype(F32)
    cconv_new = c_tail[:, CC_PAD - (CCONV - 1):]
    ffn_new = jnp.concatenate([ta[:, -1, 8 - (FFN_CONV - 1):], tb[:, -1, 8 - (FFN_CONV - 1):]],
                              axis=-1)
    return x2, kv_new, h_last.reshape(n, D_RNN), rnn_buf_new, cconv_new, ffn_new


def _sample_layer(x, w, layer, rnn_h, rnn_buf, cconv_buf, ffn_buf, caches, tiles):
    m, _, d = x.shape
    x2d = x.reshape(m, d)
    z, zqkv = _inproj(x2d, w['norm1_g'], w['w_in'], m, tiles['in_tn'])
    h, c, rnew, cnew = _state_sample(
        z, rnn_h, rnn_buf, cconv_buf, w['rnn_conv_w'], w['rnn_conv_b'], w['rnn_wa'], w['rnn_wx'],
        w['rnn_ba'], w['rnn_bx'], w['rnn_lambda'], w['cconv_w'], w['cconv_b'],
        w['cnorm_g'], w['cnorm_b'], tiles['smp_state_nb'])
    qkv = zqkv.transpose(1, 0, 2).reshape(m, N_GROUPS, 3, H_G, HD)
    o_list, lse_list, kv_new = [], [], []
    for gi in range(N_GROUPS):
        o, lse, kvn = _attn_sample(qkv, caches[gi], layer, w['q_norm_g'][gi], w['k_norm_g'][gi],
                                   gi, tiles['smp_attn_nb'])
        o_list.append(o.transpose(1, 0, 2))
        lse_list.append(jnp.pad(lse[:, :, 0], ((0, 0), (0, HD - H_G))))
        kv_new.append(kvn.reshape(m, 1, 2, H_G, HD))
    x1 = _merge(x2d, z, COL_GATE // (N_BRANCH * D_MODEL), h.astype(BF16), o_list, lse_list, c,
                w['w_br_rnn'], w['w_br_attn'], w['w_br_conv'], w['w_o'], m, False)
    x2, ffn_new = _ffn_sample(x1, w['norm2_g'], ffn_buf, w['ffn_up'], w['ffn_down'],
                              w['ffn_conv_w'], w['ffn_conv_b'], tiles['ffn_hc'])
    return x2.reshape(m, 1, d), kv_new, h, rnew, cnew, ffn_new


def _tiles(s):
    return dict(in_tm=min(1024, s), in_tn=1536, rnn_tm=min(1024, s), att_tt=2048,
                mrg_tm=min(512, s), ffn_tm=min(1024, s), ffn_hc=1024,
                smp_state_nb=32, smp_attn_nb=8)


def kernel(x_prompt, x_sample, cache_kv_w128, cache_kv_w512, cache_kv_w2048, state_rnn_h, state_rnn_conv, state_cconv, state_ffn_conv, norm1_g, w_in, rnn_conv_w, rnn_conv_b, rnn_wa, rnn_ba, rnn_wx, rnn_bx, rnn_lambda, q_norm_g, k_norm_g, cconv_w, cconv_b, cnorm_g, cnorm_b, w_br_rnn, w_br_attn, w_br_conv, w_o, norm2_g, ffn_up, ffn_conv_w, ffn_conv_b, ffn_down):
    params = dict(norm1_g=norm1_g, w_in=w_in, rnn_conv_w=rnn_conv_w, rnn_conv_b=rnn_conv_b,
                  rnn_wa=rnn_wa, rnn_ba=rnn_ba, rnn_wx=rnn_wx, rnn_bx=rnn_bx,
                  rnn_lambda=rnn_lambda, q_norm_g=q_norm_g, k_norm_g=k_norm_g, cconv_w=cconv_w,
                  cconv_b=cconv_b, cnorm_g=cnorm_g, cnorm_b=cnorm_b, w_br_rnn=w_br_rnn,
                  w_br_attn=w_br_attn, w_br_conv=w_br_conv, w_o=w_o, norm2_g=norm2_g,
                  ffn_up=ffn_up, ffn_conv_w=ffn_conv_w, ffn_conv_b=ffn_conv_b, ffn_down=ffn_down)
    depth = w_in.shape[0]
    assert x_sample.shape[1] == 1
    caches = (cache_kv_w128, cache_kv_w512, cache_kv_w2048)
    tiles = _tiles(x_prompt.shape[1])
    xp, xs = x_prompt, x_sample
    pk, sk = [[], [], []], [[], [], []]
    p_state = [[], [], [], []]
    s_state = [[], [], [], []]
    for l in range(depth):
        w = _prep_layer_weights({k: v[l] for k, v in params.items()})
        xp, kvp, hp, rbp, cbp, fbp = _prompt_layer(xp, w, tiles)
        xs, kvs, hs, rbs, cbs, fbs = _sample_layer(
            xs, w, l, state_rnn_h[l], state_rnn_conv[l], state_cconv[l], state_ffn_conv[l],
            caches, tiles)
        for gi in range(N_GROUPS):
            pk[gi].append(kvp[gi])
            sk[gi].append(kvs[gi])
        for lst, v in zip(p_state, (hp, rbp, cbp, fbp)):
            lst.append(v)
        for lst, v in zip(s_state, (hs, rbs, cbs, fbs)):
            lst.append(v)
    stack = lambda xs_: jnp.stack(xs_, 0)
    return (xp, xs, stack(pk[0]), stack(pk[1]), stack(pk[2]),
            stack(p_state[0]), stack(p_state[1]), stack(p_state[2]), stack(p_state[3]),
            stack(sk[0]), stack(sk[1]), stack(sk[2]),
            stack(s_state[0]), stack(s_state[1]), stack(s_state[2]), stack(s_state[3]))
```

```python
import functools

import jax
import jax.numpy as jnp
from jax import lax
from jax.experimental import pallas as pl
from jax.experimental.pallas import tpu as pltpu

F32 = jnp.float32
BF16 = jnp.bfloat16

D_MODEL = 1024
D_RNN = D_MODEL
RNN_BLOCKS = 8
RNN_BW = D_RNN // RNN_BLOCKS
RNN_CONV = 4
LRU_C = 8.0
ATT_GROUPS = ((128, 1), (512, 4), (2048, 16))
N_GROUPS = len(ATT_GROUPS)
H_G = 4
HD = 128
ATT_W = H_G * HD
ATT_BLK = 128
D_CONV = D_MODEL
CCONV = 31
D_FF = 3 * D_MODEL
FFN_CONV = 3
N_BRANCH = 3
EPS = 1e-6
NEG = -1e30

COL_GLU = 0
COL_RNN = 2 * D_CONV
COL_GATE = COL_RNN + D_RNN
COL_QKV = COL_GATE + N_BRANCH * D_MODEL
N_IN = COL_QKV + N_GROUPS * 3 * ATT_W

VMEM_LIMIT = 56 * 1024 * 1024


def _cparams(sem):
    return pltpu.CompilerParams(dimension_semantics=sem, vmem_limit_bytes=VMEM_LIMIT)


def _sigmoid(x):
    return 0.5 * jnp.tanh(0.5 * x) + 0.5


def _softplus(x):
    return jnp.maximum(x, 0.0) + jnp.log1p(jnp.exp(-jnp.abs(x)))


def _rmsnorm_rows(x, g):
    ms = jnp.mean(x * x, axis=-1, keepdims=True)
    return x * lax.rsqrt(ms + EPS) * g


def _layernorm_swish(y, g, b):
    mu = jnp.mean(y, axis=-1, keepdims=True)
    yc = y - mu
    var = jnp.mean(yc * yc, axis=-1, keepdims=True)
    yn = yc * lax.rsqrt(var + EPS) * g + b
    return yn * _sigmoid(yn)


def _inproj_kernel(x_ref, g_ref, w_ref, o_ref, qkv_ref, xn_ref, *, n_main):
    j = pl.program_id(1)

    @pl.when(j == 0)
    def _():
        xn_ref[...] = _rmsnorm_rows(x_ref[...], g_ref[...]).astype(BF16)

    res = jnp.dot(xn_ref[...], w_ref[...], preferred_element_type=F32)

    @pl.when(j < n_main)
    def _():
        o_ref[...] = res.astype(o_ref.dtype)

    @pl.when(j >= n_main)
    def _():
        for c in range(qkv_ref.shape[0]):
            qkv_ref[c] = res[:, c * HD:(c + 1) * HD]


def _inproj(x2d, g, w, tm, tn):
    t, d = x2d.shape
    n_main = COL_QKV // tn
    n_qkv = (N_IN - COL_QKV) // tn
    hpt = tn // HD
    return pl.pallas_call(
        functools.partial(_inproj_kernel, n_main=n_main),
        grid=(t // tm, n_main + n_qkv),
        in_specs=[pl.BlockSpec((tm, d), lambda i, j: (i, 0)),
                  pl.BlockSpec((1, d), lambda i, j: (0, 0)),
                  pl.BlockSpec((d, tn), lambda i, j: (0, j))],
        out_specs=[pl.BlockSpec((tm, tn), lambda i, j: (i, jnp.minimum(j, n_main - 1))),
                   pl.BlockSpec((hpt, tm, HD), lambda i, j: (jnp.maximum(j - n_main, 0), i, 0))],
        out_shape=[jax.ShapeDtypeStruct((t, COL_QKV), BF16),
                   jax.ShapeDtypeStruct((n_qkv * hpt, t, HD), F32)],
        scratch_shapes=[pltpu.VMEM((tm, d), BF16)],
        compiler_params=_cparams(("parallel", "arbitrary")),
        name="inproj",
    )(x2d, g, w)


def _lru_coeffs(u, r_pre, i_pre, ba, bx, sp):
    r = _sigmoid(r_pre + ba)
    ig = _sigmoid(i_pre + bx)
    log_a = (-LRU_C * r) * sp
    a = jnp.exp(log_a)
    mult = jnp.sqrt(-jnp.tanh(log_a) * (a * a + 1.0))
    return a, mult * (ig * u)


RNN_RC = 256
RNN_NB = 4


def _rnn_prompt_kernel(x_ref, cw_ref, cb_ref, wa_ref, wx_ref, ba_ref, bx_ref, lam_ref,
                       h_ref, hlast_ref, ext_ref, hp_ref, hcar_ref, *, tm):
    i = pl.program_id(1)
    nch = RNN_RC // 64
    rpv = RNN_RC // 8

    @pl.when(i == 0)
    def _():
        ext_ref[:, 0:8, :] = jnp.zeros((RNN_BLOCKS, 8, RNN_BW), F32)
        hcar_ref[...] = jnp.zeros((1, D_RNN), F32)

    for b in range(RNN_BLOCKS):
        ext_ref[b, 8:8 + tm, :] = x_ref[:, b * RNN_BW:(b + 1) * RNN_BW].astype(F32)

    row = lax.broadcasted_iota(jnp.int32, (8, RNN_BW), 0)

    for b0 in range(0, RNN_BLOCKS, RNN_NB):
        bs = tuple(range(b0, b0 + RNN_NB))
        cs = {b: slice(b * RNN_BW, (b + 1) * RNN_BW) for b in bs}
        sp = {b: _softplus(-lam_ref[:, cs[b]]) for b in bs}
        cb = {b: cb_ref[:, cs[b]] for b in bs}
        ba = {b: ba_ref[:, cs[b]] for b in bs}
        bx = {b: bx_ref[:, cs[b]] for b in bs}
        cw = {b: [cw_ref[k:k + 1, cs[b]] for k in range(RNN_CONV)] for b in bs}
        wa = {b: wa_ref[b] for b in bs}
        wx = {b: wx_ref[b] for b in bs}

        def gate_chunk(c, cars, bs=bs, cs=cs, sp=sp, cb=cb, ba=ba, bx=bx, cw=cw, wa=wa, wx=wx):
            r0 = pl.multiple_of(c * RNN_RC, RNN_RC)
            u_all = {}
            for b in bs:
                slabs = [ext_ref[b, pl.ds(r0 + (8 - (RNN_CONV - 1) + wv), rpv, stride=8), :]
                         for wv in range(8 + RNN_CONV - 1)]
                us = []
                for v in range(8):
                    u = cb[b]
                    for k in range(RNN_CONV):
                        u = u + cw[b][k] * slabs[v + k]
                    us.append(u)
                u_all[b] = jnp.concatenate(us, axis=0)
            ub = {b: u_all[b].astype(BF16) for b in bs}
            r_pre = {b: jnp.dot(ub[b], wa[b], preferred_element_type=F32) for b in bs}
            i_pre = {b: jnp.dot(ub[b], wx[b], preferred_element_type=F32) for b in bs}
            coef = {b: _lru_coeffs(u_all[b], r_pre[b], i_pre[b], ba[b], bx[b], sp[b]) for b in bs}
            hs = {b: [] for b in bs}
            ps = {b: [] for b in bs}
            for v in range(8):
                for b in bs:
                    a = coef[b][0][v * rpv:(v + 1) * rpv]
                    uu = coef[b][1][v * rpv:(v + 1) * rpv]
                    hs[b].append(uu if v == 0 else a * hs[b][-1] + uu)
                    ps[b].append(a if v == 0 else a * ps[b][-1])
            car = dict(zip(bs, cars))
            cins = {b: [] for b in bs}
            for q in range(nch):
                for b in bs:
                    pe, he = _sublane_scan(ps[b][-1][8 * q:8 * q + 8], hs[b][-1][8 * q:8 * q + 8], row)
                    e = he + pe * car[b]
                    cins[b].append(jnp.where(row == 0, car[b], pltpu.roll(e, 1, axis=0)))
                    car[b] = e[7:8, :]
            cin = {b: jnp.concatenate(cins[b], axis=0) for b in bs}
            for v in range(8):
                for b in bs:
                    hp_ref[b, v * rpv:(v + 1) * rpv, :] = hs[b][v] + ps[b][v] * cin[b]
            for j in range(0, rpv, 2):
                for b in bs:
                    t0 = hp_ref[b, pl.ds(j, 8, stride=rpv), :]
                    t1 = hp_ref[b, pl.ds(j + 1, 8, stride=rpv), :]
                    dst = pl.multiple_of(r0 + 8 * j, 16)
                    h_ref[pl.ds(dst, 16), cs[b]] = jnp.concatenate([t0, t1], axis=0).astype(h_ref.dtype)
            return tuple(car[b] for b in bs)

        cars = lax.fori_loop(0, tm // RNN_RC, gate_chunk, tuple(hcar_ref[:, cs[b]] for b in bs))
        for b, car in zip(bs, cars):
            hcar_ref[:, cs[b]] = car
            hlast_ref[:, cs[b]] = car
            ext_ref[b, 0:8, :] = ext_ref[b, tm:tm + 8, :]


def _sublane_scan(p, h, row):
    for d in (1, 2, 4):
        hr = pltpu.roll(h, d, axis=0)
        pr = pltpu.roll(p, d, axis=0)
        ok = row >= d
        h = jnp.where(ok, h + p * hr, h)
        p = jnp.where(ok, p * pr, p)
    return p, h


def _rnn_prompt(z, cw, cb, wa, wx, ba, bx, lam, tm):
    n, s, _ = z.shape
    assert tm % RNN_RC == 0
    cblk = 0
    vec = pl.BlockSpec((1, D_RNN), lambda b, i: (0, 0))
    blkw = pl.BlockSpec((RNN_BLOCKS, RNN_BW, RNN_BW), lambda b, i: (0, 0, 0))
    return pl.pallas_call(
        functools.partial(_rnn_prompt_kernel, tm=tm),
        grid=(n, s // tm),
        in_specs=[pl.BlockSpec((None, tm, D_RNN), lambda b, i: (b, i, cblk)),
                  pl.BlockSpec((RNN_CONV, D_RNN), lambda b, i: (0, 0)),
                  vec, blkw, blkw, vec, vec, vec],
        out_specs=[pl.BlockSpec((None, tm, D_RNN), lambda b, i: (b, i, 0)),
                   pl.BlockSpec((None, 1, D_RNN), lambda b, i: (b, 0, 0))],
        out_shape=[jax.ShapeDtypeStruct((n, s, D_RNN), BF16),
                   jax.ShapeDtypeStruct((n, 1, D_RNN), F32)],
        scratch_shapes=[pltpu.VMEM((RNN_BLOCKS, tm + 8, RNN_BW), F32),
                        pltpu.VMEM((RNN_BLOCKS, RNN_RC, RNN_BW), F32),
                        pltpu.VMEM((1, D_RNN), F32)],
        compiler_params=_cparams(("parallel", "arbitrary")),
        name="rnn_prompt",
    )(z, cw, cb, wa, wx, ba, bx, lam)


ATT_UNROLL = 2


def _attn_prompt_kernel(q_ref, k_ref, v_ref, gq_ref, gk_ref,
                        o_ref, lse_ref, kn_ref, ks_ref, vs_ref, knt_ref, *, dil, tt, nblk_total):
    ti = pl.program_id(1)
    nblk = tt // (ATT_BLK * dil)
    units = tt // ATT_BLK
    sb = tt // nblk

    @pl.when(ti == 0)
    def _():
        ks_ref[...] = jnp.zeros(ks_ref.shape, BF16)
        vs_ref[...] = jnp.zeros(vs_ref.shape, BF16)

    sa = lax.broadcasted_iota(jnp.int32, (ATT_BLK, ATT_BLK), 0)
    sk = lax.broadcasted_iota(jnp.int32, (ATT_BLK, ATT_BLK), 1)
    if dil == 4:
        half = ATT_BLK // 2
        pos = lambda a: 2 * (a & (half - 1)) + (a >> (half.bit_length() - 1))
    else:
        pos = lambda a: a
    own_ok = pos(sk) <= pos(sa)
    prev_band = pos(sk) >= pos(sa)
    lane = lax.broadcasted_iota(jnp.int32, (ATT_BLK, HD), 1)
    ones = jnp.ones((ATT_BLK, HD), BF16)
    gq = gq_ref[...] * (HD ** -0.5)
    gk = gk_ref[...]
    dn = (((1,), (1,)), ((), ()))
    upb = units // nblk
    sh = upb.bit_length() - 1

    def load(ref, h, jb, r):
        base = jb * sb + r
        if dil == 1:
            return ref[h, pl.ds(pl.multiple_of(base, ATT_BLK), ATT_BLK), :]
        if dil == 4:
            return jnp.concatenate([ref[h, pl.ds(base, ATT_BLK // 2, stride=8), :],
                                    ref[h, pl.ds(base + 4, ATT_BLK // 2, stride=8), :]], axis=0)
        return ref[h, pl.ds(base, ATT_BLK, stride=dil), :]

    def store_natural(ref, h, r, val):
        if dil == 1:
            ref[h] = val
        elif dil == 4:
            ref[h, pl.ds(r, ATT_BLK // 2, stride=8), :] = val[:ATT_BLK // 2]
            ref[h, pl.ds(r + 4, ATT_BLK // 2, stride=8), :] = val[ATT_BLK // 2:]
        else:
            ref[h, pl.ds(r, ATT_BLK, stride=dil), :] = val

    def unit_group(g, carry):
        us = [g * ATT_UNROLL + e for e in range(ATT_UNROLL)]
        jb = [u >> sh for u in us]
        r = [u & (upb - 1) for u in us]
        gblk = [ti * nblk + j for j in jb]
        par = [gb & 1 for gb in gblk]
        prev_ok = [jnp.logical_and(prev_band, gb > 0) for gb in gblk]
        dst = [_unit_rows(dil, jb[e], sb, r[e]) for e in range(ATT_UNROLL)]
        eh = [(e, h) for e in range(ATT_UNROLL) for h in range(H_G)]
        qb = {k: _rmsnorm_rows(load(q_ref, k[1], jb[k[0]], r[k[0]]), gq).astype(BF16) for k in eh}
        kn = {k: _rmsnorm_rows(load(k_ref, k[1], jb[k[0]], r[k[0]]), gk) for k in eh}
        for e, h in eh:
            knt_ref[e, h] = kn[e, h]
        kb = {k: kn[k].astype(BF16) for k in eh}
        vb = {k: jnp.concatenate([load(v_ref, k[1], jb[k[0]], r[k[0]]).astype(BF16), ones], axis=1)
              for k in eh}
        kprev, vprev = {}, {}
        for e, h in eh:
            if upb == 1 and e > 0:
                kprev[e, h], vprev[e, h] = kb[e - 1, h], vb[e - 1, h]
            else:
                kprev[e, h] = ks_ref[r[e], h, 1 - par[e]]
                vprev[e, h] = vs_ref[r[e], h, 1 - par[e]]
        for e, h in eh:
            ks_ref[r[e], h, par[e]] = kb[e, h]
            vs_ref[r[e], h, par[e]] = vb[e, h]
        s = {k: lax.dot_general(qb[k], jnp.concatenate([kprev[k], kb[k]], axis=0), dn,
                                preferred_element_type=F32) for k in eh}
        s_prev = {k: jnp.where(prev_ok[k[0]], s[k][:, :ATT_BLK], NEG) for k in eh}
        s_own = {k: jnp.where(own_ok, s[k][:, ATT_BLK:], NEG) for k in eh}
        m = {k: jnp.max(jnp.maximum(s_prev[k], s_own[k]), axis=-1, keepdims=True) for k in eh}
        p = {k: jnp.concatenate([jnp.exp(s_prev[k] - m[k]), jnp.exp(s_own[k] - m[k])],
                                axis=1).astype(BF16) for k in eh}
        acc = {k: jnp.dot(p[k], jnp.concatenate([vprev[k], vb[k]], axis=0),
                          preferred_element_type=F32) for k in eh}
        for e in range(ATT_UNROLL):
            lse_tile = jnp.zeros((ATT_BLK, HD), F32)
            for h in range(H_G):
                l = acc[e, h][:, HD:]
                o = acc[e, h][:, :HD] / l
                for lo, n_rows, src in dst[e]:
                    o_ref[h, pl.ds(lo, n_rows), :] = o[src:src + n_rows]
                lse_tile = jnp.where(lane == h, m[e, h] + jnp.log(l), lse_tile)
            for lo, n_rows, src in dst[e]:
                lse_ref[pl.ds(lo, n_rows), :] = lse_tile[src:src + n_rows]

        for e in range(ATT_UNROLL):
            @pl.when(gblk[e] == nblk_total - 1)
            def _(e=e):
                for h in range(H_G):
                    store_natural(kn_ref, h, r[e], knt_ref[e, h])

        return carry

    lax.fori_loop(0, units // ATT_UNROLL, unit_group, 0)


def _unit_rows(dil, jb, sb, r):
    base = jb * sb
    if dil == 1:
        return [(pl.multiple_of(base, ATT_BLK), ATT_BLK, 0)]
    if dil == 4:
        half = ATT_BLK // 2
        return [(pl.multiple_of(base + half * r, half), half, 0),
                (pl.multiple_of(base + sb // 2 + half * r, half), half, half)]
    return [(pl.multiple_of(base + ATT_BLK * r, ATT_BLK), ATT_BLK, 0)]


def _attn_prompt(zqkv, gq, gk, gi, tt):
    _, _, n, s, _ = zqkv.shape
    win, dil = ATT_GROUPS[gi]
    assert win == ATT_BLK * dil and tt % win == 0 and s % tt == 0
    upb = tt // ATT_BLK // (tt // win)
    spec = lambda c: pl.BlockSpec((None, H_G, None, tt, HD), lambda b, i, c=c: (c, 0, b, i, 0))
    gspec = pl.BlockSpec((1, HD), lambda b, i: (0, 0))
    hspec = pl.BlockSpec((H_G, None, tt, HD), lambda b, i: (0, b, i, 0))
    return pl.pallas_call(
        functools.partial(_attn_prompt_kernel, dil=dil, tt=tt, nblk_total=s // win),
        grid=(n, s // tt),
        in_specs=[spec(3 * gi), spec(3 * gi + 1), spec(3 * gi + 2), gspec, gspec],
        out_specs=[hspec, pl.BlockSpec((None, tt, HD), lambda b, i: (b, i, 0)),
                   pl.BlockSpec((H_G, None, win, HD), lambda b, i: (0, b, 0, 0))],
        out_shape=[jax.ShapeDtypeStruct((H_G, n, s, HD), F32),
                   jax.ShapeDtypeStruct((n, s, HD), F32),
                   jax.ShapeDtypeStruct((H_G, n, win, HD), F32)],
        scratch_shapes=[pltpu.VMEM((upb, H_G, 2, ATT_BLK, HD), BF16),
                        pltpu.VMEM((upb, H_G, 2, ATT_BLK, 2 * HD), BF16),
                        pltpu.VMEM((ATT_UNROLL, H_G, ATT_BLK, HD), F32)],
        compiler_params=_cparams(("parallel", "arbitrary")),
        name=f"attn_prompt_g{gi}",
    )(zqkv, zqkv, zqkv, gq, gk)


CC_PAD = 32
CC_RC = 64
CC_LB = 128


FIN_TN = 1536
FIN_PW = 256
FIN_CONV_STEPS = ((0, 3), (3, 6), (6, 9), (9, 12), (12, 16))


def _fused_inproj_kernel(x_ref, g_ref, w_ref, ccw_ref, ccb_ref, ng_ref, nb_ref,
                         rnn_ref, gate_ref, qkv_ref, c_ref, tail_ref,
                         xn_ref, ga_ref, ext_ref, y_ref, *, tm):
    i = pl.program_id(1)
    j = pl.program_id(2)
    npc = FIN_TN // FIN_PW
    nlb = D_CONV // CC_LB
    nv = CC_RC // 8
    off = CC_PAD - (CCONV - 1)

    def piece(p):
        return jnp.dot(xn_ref[...], w_ref[:, p * FIN_PW:(p + 1) * FIN_PW],
                       preferred_element_type=F32)

    def run(sink, units):
        per = -(-len(units) // npc)
        for p in range(npc):
            sink(p, piece(p))
            for unit in units[p * per:(p + 1) * per]:
                unit()

    def glu_to_ext(lb, gb):
        ls = slice(lb * CC_LB, (lb + 1) * CC_LB)
        ext_ref[lb, CC_PAD:CC_PAD + tm, :] = ga_ref[:, ls] * _sigmoid(gb)

    def conv_unit(q, lb):
        ls = slice(lb * CC_LB, (lb + 1) * CC_LB)
        r0 = q * CC_RC
        slabs = {}

        def slab(wv):
            if wv not in slabs:
                slabs[wv] = ext_ref[lb, pl.ds(r0 + off + wv, 8, stride=nv), :]
            return slabs[wv]

        acc = [jnp.zeros((8, CC_LB), F32) + ccb_ref[:, ls]] * nv
        for k in range(CCONV):
            wk = ccw_ref[k:k + 1, ls]
            for v in range(nv):
                acc[v] = acc[v] + wk * slab(v + k)
        y_ref[q % 2, lb] = jnp.concatenate(acc, axis=0)

    def norm_unit(q):
        y = jnp.concatenate(
            [jnp.concatenate([y_ref[q % 2, lb, pl.ds(t, 8, stride=nv), :] for t in range(nv)], axis=0)
             for lb in range(nlb)], axis=-1)
        c_ref[q * CC_RC:(q + 1) * CC_RC, :] = _layernorm_swish(
            y, ng_ref[...], nb_ref[...]).astype(c_ref.dtype)

    def conv_units(q_lo, q_hi):
        units = []
        for q in range(q_lo, q_hi):
            units += [functools.partial(conv_unit, q, lb) for lb in range(nlb)]
            units.append(functools.partial(norm_unit, q))
        return units

    @pl.when(j == 0)
    def _():
        xn_ref[...] = _rmsnorm_rows(x_ref[...], g_ref[...]).astype(BF16)

        @pl.when(i == 0)
        def _():
            ext_ref[:, 0:CC_PAD, :] = jnp.zeros((nlb, CC_PAD, CC_LB), F32)

        def sink(p, val):
            if p * FIN_PW < D_CONV:
                ga_ref[:, p * FIN_PW:(p + 1) * FIN_PW] = val
            else:
                for h in range(FIN_PW // CC_LB):
                    lb = (p * FIN_PW - D_CONV) // CC_LB + h
                    glu_to_ext(lb, val[:, h * CC_LB:(h + 1) * CC_LB])

        run(sink, [])

    @pl.when(j == 1)
    def _():
        n_gb = (2 * D_CONV - FIN_TN) // FIN_PW

        def sink(p, val):
            if p < n_gb:
                for h in range(FIN_PW // CC_LB):
                    lb = (FIN_TN - D_CONV + p * FIN_PW) // CC_LB + h
                    glu_to_ext(lb, val[:, h * CC_LB:(h + 1) * CC_LB])
            else:
                c0 = (p - n_gb) * FIN_PW
                rnn_ref[:, c0:c0 + FIN_PW] = val.astype(rnn_ref.dtype)

        run(sink, [])
        for lb in range(nlb):
            tail_ref[:, lb * CC_LB:(lb + 1) * CC_LB] = ext_ref[lb, tm:tm + CC_PAD, :]

    def gate_sink(p, val):
        gate_ref[:, p * FIN_PW:(p + 1) * FIN_PW] = val.astype(gate_ref.dtype)

    def qkv_sink(p, val):
        for h in range(FIN_PW // HD):
            qkv_ref[p * (FIN_PW // HD) + h] = val[:, h * HD:(h + 1) * HD]

    for step, (q_lo, q_hi) in enumerate(FIN_CONV_STEPS):
        @pl.when(j == 2 + step)
        def _(step=step, q_lo=q_lo, q_hi=q_hi):
            run(gate_sink if step < 2 else qkv_sink, conv_units(q_lo, q_hi))

    @pl.when(j == pl.num_programs(2) - 1)
    def _():
        for lb in range(nlb):
            ext_ref[lb, 0:CC_PAD, :] = ext_ref[lb, tm:tm + CC_PAD, :]


def _fused_inproj(x, g, w, ccw, ccb, ng, nb, tm):
    n, s, d = x.shape
    nj = N_IN // FIN_TN
    assert (COL_RNN + D_RNN, COL_GATE, COL_QKV) == (2 * FIN_TN, 2 * FIN_TN, 4 * FIN_TN)
    assert nj == 2 + len(FIN_CONV_STEPS) and tm == FIN_CONV_STEPS[-1][1] * CC_RC
    hpt = FIN_TN // HD
    vec = pl.BlockSpec((1, D_CONV), lambda b, i, j: (0, 0))
    return pl.pallas_call(
        functools.partial(_fused_inproj_kernel, tm=tm),
        grid=(n, s // tm, nj),
        in_specs=[pl.BlockSpec((None, tm, d), lambda b, i, j: (b, i, 0)),
                  pl.BlockSpec((1, d), lambda b, i, j: (0, 0)),
                  pl.BlockSpec((d, FIN_TN), lambda b, i, j: (0, j)),
                  pl.BlockSpec((CCONV, D_CONV), lambda b, i, j: (0, 0)),
                  vec, vec, vec],
        out_specs=[pl.BlockSpec((None, tm, D_RNN), lambda b, i, j: (b, i, 0)),
                   pl.BlockSpec((None, tm, FIN_TN), lambda b, i, j: (b, i, jnp.clip(j - 2, 0, 1))),
                   pl.BlockSpec((hpt, None, tm, HD), lambda b, i, j: (jnp.clip(j - 4, 0, 2), b, i, 0)),
                   pl.BlockSpec((None, tm, D_CONV), lambda b, i, j: (b, i, 0)),
                   pl.BlockSpec((None, CC_PAD, D_CONV), lambda b, i, j: (b, 0, 0))],
        out_shape=[jax.ShapeDtypeStruct((n, s, D_RNN), BF16),
                   jax.ShapeDtypeStruct((n, s, N_BRANCH * D_MODEL), BF16),
                   jax.ShapeDtypeStruct((N_GROUPS * 3 * H_G, n, s, HD), F32),
                   jax.ShapeDtypeStruct((n, s, D_CONV), BF16),
                   jax.ShapeDtypeStruct((n, CC_PAD, D_CONV), F32)],
        scratch_shapes=[pltpu.VMEM((tm, d), BF16),
                        pltpu.VMEM((tm, D_CONV), F32),
                        pltpu.VMEM((D_CONV // CC_LB, tm + CC_PAD, CC_LB), F32),
                        pltpu.VMEM((2, D_CONV // CC_LB, CC_RC, CC_LB), F32)],
        compiler_params=_cparams(("parallel", "arbitrary", "arbitrary")),
        name="inproj_cconv",
    )(x, g, w, ccw, ccb, ng, nb)


MERGE_CB = 512


def _merge_kernel(x_ref, g_ref, h_ref, o0_ref, o1_ref, o2_ref, l0_ref, l1_ref, l2_ref, c_ref,
                  wr_ref, wa_ref, wc_ref, wo_ref, y_ref, att_ref, mrg_ref, part_ref,
                  *, unit_major, tm):
    o_refs = (o0_ref, o1_ref, o2_ref)
    l_refs = (l0_ref, l1_ref, l2_ref)
    i = pl.program_id(0)

    def rows(ref, lead, g):
        pre = () if lead is None else (lead,)
        dil = ATT_GROUPS[g][1]
        if not unit_major or dil == 1:
            return ref[pre + (slice(None), slice(None))]
        nv = tm // 8
        if dil == 4:
            starts = [(j, tm // 8) for j in range(nv)]
        else:
            sb = ATT_GROUPS[g][0]
            q = (i % (sb // tm)) * (tm // 16)
            starts = [(q + ((sb // 2) * (j % 2) + j // 2), ATT_BLK) for j in range(nv)]
        return jnp.concatenate(
            [ref[pre + (pl.ds(st, 8, stride=sd), slice(None))] for st, sd in starts], axis=0)

    lses = [rows(l_refs[g], None, g) for g in range(N_GROUPS)]

    def combine_head(h):
        lh = [l[:, h:h + 1] for l in lses]
        m = jnp.maximum(jnp.maximum(lh[0], lh[1]), lh[2])
        e = [jnp.exp(v - m) for v in lh]
        den = e[0] + e[1] + e[2]
        att = (e[0] / den) * rows(o_refs[0], h, 0)
        att = att + (e[1] / den) * rows(o_refs[1], h, 1)
        att = att + (e[2] / den) * rows(o_refs[2], h, 2)
        att_ref[:, h * HD:(h + 1) * HD] = att.astype(BF16)

    def gate(br, j):
        c0 = br * D_MODEL + j * MERGE_CB
        return _sigmoid(g_ref[:, c0:c0 + MERGE_CB].astype(F32))

    ncb = D_MODEL // MERGE_CB
    heads = list(range(H_G))
    for j in range(ncb):
        cs = slice(j * MERGE_CB, (j + 1) * MERGE_CB)
        part = gate(0, j) * jnp.dot(h_ref[...], wr_ref[:, cs], preferred_element_type=F32)
        for h in heads[2 * j * H_G // (2 * ncb):(2 * j + 1) * H_G // (2 * ncb)]:
            combine_head(h)
        part = part + gate(2, j) * jnp.dot(c_ref[...], wc_ref[:, cs], preferred_element_type=F32)
        for h in heads[(2 * j + 1) * H_G // (2 * ncb):(2 * j + 2) * H_G // (2 * ncb)]:
            combine_head(h)
        part_ref[:, cs] = part
    for j in range(ncb):
        cs = slice(j * MERGE_CB, (j + 1) * MERGE_CB)
        acc = part_ref[:, cs] + gate(1, j) * jnp.dot(att_ref[...], wa_ref[:, cs],
                                                      preferred_element_type=F32)
        mrg_ref[:, cs] = acc.astype(BF16)

    y_ref[...] = x_ref[...] + jnp.dot(mrg_ref[...], wo_ref[...], preferred_element_type=F32)


def _merge(x2d, z2d, cg, h2d, o_list, lse_list, c2d, wr, wa, wc, wo, tm, unit_major):
    t, d = x2d.shape
    row = lambda w: pl.BlockSpec((tm, w), lambda i: (i, 0))
    full = lambda a: pl.BlockSpec(a.shape, lambda i: (0, 0))
    ohead = pl.BlockSpec((H_G, tm, HD), lambda i: (0, i, 0))
    o_specs, l_specs = [], []
    for win, dil in ATT_GROUPS:
        if unit_major and dil > 1:
            assert win % tm == 0 and (dil != 4 or win == tm) and tm % 16 == 0
            per = win // tm
            o_specs.append(pl.BlockSpec((H_G, win, HD), lambda i, per=per: (0, i // per, 0)))
            l_specs.append(pl.BlockSpec((win, HD), lambda i, per=per: (i // per, 0)))
        else:
            o_specs.append(ohead)
            l_specs.append(row(HD))
    return pl.pallas_call(
        functools.partial(_merge_kernel, unit_major=unit_major, tm=tm),
        grid=(t // tm,),
        in_specs=[row(d),
                  pl.BlockSpec((tm, N_BRANCH * D_MODEL), lambda i: (i, cg)),
                  row(D_RNN), *o_specs, *l_specs,
                  row(D_CONV), full(wr), full(wa), full(wc), full(wo)],
        out_specs=row(d),
        out_shape=jax.ShapeDtypeStruct((t, d), F32),
        scratch_shapes=[pltpu.VMEM((tm, ATT_W), BF16), pltpu.VMEM((tm, D_MODEL), BF16),
                        pltpu.VMEM((tm, D_MODEL), F32)],
        compiler_params=_cparams(("parallel",)),
        name="merge",
    )(x2d, z2d, h2d, *o_list, *lse_list, c2d, wr, wa, wc, wo)


def _gelu(x):
    return 0.5 * x * (1.0 + jnp.tanh(0.7978845608028654 * (x + 0.044715 * (x * x * x))))


def _ffn_conv3(up, hist_ref, j, w_ref, b_ref, ext_ref, ps, tm):
    ext_ref[0:8, ps] = hist_ref[j, :, ps]
    ext_ref[8:8 + tm, ps] = up
    y = b_ref[:, ps] + w_ref[2:3, ps] * up
    y = y + w_ref[1:2, ps] * ext_ref[7:7 + tm, ps]
    y = y + w_ref[0:1, ps] * ext_ref[6:6 + tm, ps]
    hist_ref[j, :, ps] = ext_ref[tm:tm + 8, ps]
    return y


def _ffn_prompt_kernel(x_ref, g_ref, wua_ref, wub_ref, wd_ref, cwa_ref, cwb_ref, cba_ref, cbb_ref,
                       y_ref, ta_ref, tb_ref, xn_ref, acc_ref, ha_ref, hb_ref, ea_ref, eb_ref, *, tm):
    i = pl.program_id(1)
    j = pl.program_id(2)
    nj = pl.num_programs(2)

    @pl.when(j == 0)
    def _():
        x = x_ref[...]
        xn_ref[...] = _rmsnorm_rows(x, g_ref[...]).astype(BF16)
        acc_ref[...] = x

    @pl.when(i == 0)
    def _():
        ha_ref[j] = jnp.zeros(ha_ref.shape[1:], F32)
        hb_ref[j] = jnp.zeros(hb_ref.shape[1:], F32)

    xn = xn_ref[...]
    al = slice(None)
    up_a = jnp.dot(xn, wua_ref[...], preferred_element_type=F32)
    ga = _gelu(_ffn_conv3(up_a, ha_ref, j, cwa_ref, cba_ref, ea_ref, al, tm))
    up_b = jnp.dot(xn, wub_ref[...], preferred_element_type=F32)
    fb = _ffn_conv3(up_b, hb_ref, j, cwb_ref, cbb_ref, eb_ref, al, tm)
    ta_ref[...] = ea_ref[tm:tm + 8, :]
    tb_ref[...] = eb_ref[tm:tm + 8, :]
    gact = (ga * fb).astype(BF16)
    acc_ref[...] += jnp.dot(gact, wd_ref[...], preferred_element_type=F32)

    @pl.when(j == nj - 1)
    def _():
        y_ref[...] = acc_ref[...]


def _ffn_prompt(x, g, wu, wd, cw, cb, tm, hc):
    n, s, d = x.shape
    nj = D_FF // hc
    return pl.pallas_call(
        functools.partial(_ffn_prompt_kernel, tm=tm),
        grid=(n, s // tm, nj),
        in_specs=[pl.BlockSpec((None, tm, d), lambda b, i, j: (b, i, 0)),
                  pl.BlockSpec((1, d), lambda b, i, j: (0, 0)),
                  pl.BlockSpec((d, hc), lambda b, i, j: (0, j)),
                  pl.BlockSpec((d, hc), lambda b, i, j: (0, nj + j)),
                  pl.BlockSpec((hc, d), lambda b, i, j: (j, 0)),
                  pl.BlockSpec((FFN_CONV, hc), lambda b, i, j: (0, j)),
                  pl.BlockSpec((FFN_CONV, hc), lambda b, i, j: (0, nj + j)),
                  pl.BlockSpec((1, hc), lambda b, i, j: (0, j)),
                  pl.BlockSpec((1, hc), lambda b, i, j: (0, nj + j))],
        out_specs=[pl.BlockSpec((None, tm, d), lambda b, i, j: (b, i, 0)),
                   pl.BlockSpec((None, None, 8, hc), lambda b, i, j: (b, i, 0, j)),
                   pl.BlockSpec((None, None, 8, hc), lambda b, i, j: (b, i, 0, j))],
        out_shape=[jax.ShapeDtypeStruct((n, s, d), F32),
                   jax.ShapeDtypeStruct((n, s // tm, 8, D_FF), F32),
                   jax.ShapeDtypeStruct((n, s // tm, 8, D_FF), F32)],
        scratch_shapes=[pltpu.VMEM((tm, d), BF16),
                        pltpu.VMEM((tm, d), F32),
                        pltpu.VMEM((nj, 8, hc), F32),
                        pltpu.VMEM((nj, 8, hc), F32),
                        pltpu.VMEM((tm + 8, hc), F32),
                        pltpu.VMEM((tm + 8, hc), F32)],
        compiler_params=_cparams(("parallel", "arbitrary", "arbitrary")),
        name="ffn_prompt",
    )(x, g, wu, wu, wd, cw, cw, cb, cb)


def _state_sample_kernel(z_ref, rh_ref, rbuf_ref, cbuf_ref,
                         rcw_ref, rcb_ref, wa_ref, wx_ref, ba_ref, bx_ref, lam_ref,
                         ccw_ref, ccb_ref, ng_ref, nb_ref,
                         h_ref, c_ref, rnew_ref, cnew_ref):
    x_rnn = z_ref[:, COL_RNN:COL_RNN + D_RNN].astype(F32)
    u = rcb_ref[...] + rcw_ref[RNN_CONV - 1:RNN_CONV, :] * x_rnn
    for k in range(RNN_CONV - 1):
        row = rbuf_ref[k]
        u = u + rcw_ref[k:k + 1, :] * row
        if k > 0:
            rnew_ref[k - 1] = row
    rnew_ref[RNN_CONV - 2] = x_rnn
    ub = u.astype(BF16)
    r_pre = jnp.concatenate(
        [jnp.dot(ub[:, b * RNN_BW:(b + 1) * RNN_BW], wa_ref[b], preferred_element_type=F32)
         for b in range(RNN_BLOCKS)], axis=-1)
    i_pre = jnp.concatenate(
        [jnp.dot(ub[:, b * RNN_BW:(b + 1) * RNN_BW], wx_ref[b], preferred_element_type=F32)
         for b in range(RNN_BLOCKS)], axis=-1)
    a, uu = _lru_coeffs(u, r_pre, i_pre, ba_ref[...], bx_ref[...], _softplus(-lam_ref[...]))
    h_ref[...] = uu + a * rh_ref[...]

    ga = z_ref[:, COL_GLU:COL_GLU + D_CONV].astype(F32)
    gb = z_ref[:, COL_GLU + D_CONV:COL_GLU + 2 * D_CONV].astype(F32)
    c_in = ga * _sigmoid(gb)
    y = ccb_ref[...] + ccw_ref[CCONV - 1:CCONV, :] * c_in
    for k in range(CCONV - 1):
        row = cbuf_ref[k]
        y = y + ccw_ref[k:k + 1, :] * row
        if k > 0:
            cnew_ref[k - 1] = row
    cnew_ref[CCONV - 2] = c_in
    c_ref[...] = _layernorm_swish(y, ng_ref[...], nb_ref[...]).astype(c_ref.dtype)


def _state_sample(z, rh, rbuf, cbuf, rcw, rcb, wa, wx, ba, bx, lam, ccw, ccb, ng, nbias, nb):
    m = z.shape[0]
    wz = COL_GATE
    row = lambda w: pl.BlockSpec((nb, w), lambda i: (i, 0))
    full = lambda a: pl.BlockSpec(a.shape, lambda i: (0,) * a.ndim)
    rspec = pl.BlockSpec((RNN_CONV - 1, nb, D_RNN), lambda i: (0, i, 0))
    cspec = pl.BlockSpec((CCONV - 1, nb, D_CONV), lambda i: (0, i, 0))
    rbuf = rbuf.transpose(1, 0, 2)
    cbuf = cbuf.transpose(1, 0, 2)
    h, c, rnew, cnew = pl.pallas_call(
        _state_sample_kernel,
        grid=(m // nb,),
        in_specs=[row(wz), row(D_RNN), rspec, cspec,
                  full(rcw), full(rcb), full(wa), full(wx), full(ba), full(bx), full(lam),
                  full(ccw), full(ccb), full(ng), full(nbias)],
        out_specs=[row(D_RNN), row(D_CONV), rspec, cspec],
        out_shape=[jax.ShapeDtypeStruct((m, D_RNN), F32),
                   jax.ShapeDtypeStruct((m, D_CONV), BF16),
                   jax.ShapeDtypeStruct((RNN_CONV - 1, m, D_RNN), F32),
                   jax.ShapeDtypeStruct((CCONV - 1, m, D_CONV), F32)],
        compiler_params=_cparams(("parallel",)),
        name="state_sample",
    )(z, rh, rbuf, cbuf, rcw, rcb, wa, wx, ba, bx, lam, ccw, ccb, ng, nbias)
    return h, c, rnew.transpose(1, 0, 2), cnew.transpose(1, 0, 2)


def _attn_sample_kernel(qkv_ref, kv_ref, gq_ref, gk_ref, o_ref, lse_ref, kvn_ref):
    log2e = 1.4426950408889634
    qn = _rmsnorm_rows(qkv_ref[:, 0], gq_ref[...] * (HD ** -0.5 * log2e))
    kn = _rmsnorm_rows(qkv_ref[:, 1], gk_ref[...])
    vn = qkv_ref[:, 2]
    kvn_ref[:, 0] = kn
    kvn_ref[:, 1] = vn
    kc = kv_ref[:, :, 0]
    vc = kv_ref[:, :, 1]
    s_new = jnp.sum(qn * kn, axis=-1, keepdims=True)
    s_buf = jnp.sum(qn[:, None] * kc, axis=-1, keepdims=True)
    m = jnp.maximum(jnp.max(s_buf, axis=1), s_new)
    p_new = jnp.exp2(s_new - m)
    p_buf = jnp.exp2(s_buf - m[:, None])
    l = p_new + jnp.sum(p_buf, axis=1)
    acc = p_new * vn + jnp.sum(p_buf * vc, axis=1)
    o_ref[...] = acc / l
    lse_ref[...] = jnp.broadcast_to(m * (1.0 / log2e) + jnp.log(l), lse_ref.shape)


def _attn_sample(qkv, cache, layer, gq, gk, gi, nb):
    m = qkv.shape[0]
    win, dil = ATT_GROUPS[gi]
    depth, mb, L = cache.shape[:3]
    assert L == win and L // dil == ATT_BLK and mb == m
    view = cache.reshape(depth, m, ATT_BLK, dil, 2, H_G, HD)
    hspec = pl.BlockSpec((nb, H_G, HD), lambda i: (i, 0, 0))
    gspec = pl.BlockSpec((1, 1, HD), lambda i: (0, 0, 0))
    return pl.pallas_call(
        _attn_sample_kernel,
        grid=(m // nb,),
        in_specs=[pl.BlockSpec((nb, None, 3, H_G, HD), lambda i: (i, gi, 0, 0, 0)),
                  pl.BlockSpec((None, nb, ATT_BLK, None, 2, H_G, HD),
                               lambda i: (layer, i, 0, 0, 0, 0, 0)),
                  gspec, gspec],
        out_specs=[hspec, hspec, pl.BlockSpec((nb, 2, H_G, HD), lambda i: (i, 0, 0, 0))],
        out_shape=[jax.ShapeDtypeStruct((m, H_G, HD), F32),
                   jax.ShapeDtypeStruct((m, H_G, HD), F32),
                   jax.ShapeDtypeStruct((m, 2, H_G, HD), F32)],
        compiler_params=_cparams(("parallel",)),
        name=f"attn_sample_g{gi}",
    )(qkv, view, gq.reshape(1, 1, HD), gk.reshape(1, 1, HD))


def _ffn_sample_kernel(x_ref, g_ref, buf_ref, wu_ref, wd_ref, cw_ref, cb_ref,
                       y_ref, new_ref, xn_ref, acc_ref):
    j = pl.program_id(0)
    nj = pl.num_programs(0)

    @pl.when(j == 0)
    def _():
        x = x_ref[...]
        xn_ref[...] = _rmsnorm_rows(x, g_ref[...]).astype(BF16)
        acc_ref[...] = x

    f = []
    for half in range(2):
        up = jnp.dot(xn_ref[...], wu_ref[half], preferred_element_type=F32)
        b0 = buf_ref[0, half]
        b1 = buf_ref[1, half]
        y = cb_ref[half] + cw_ref[half, FFN_CONV - 1:FFN_CONV, :] * up
        y = y + cw_ref[half, 0:1, :] * b0 + cw_ref[half, 1:2, :] * b1
        new_ref[0, half] = b1
        new_ref[1, half] = up
        f.append(y)
    gact = (_gelu(f[0]) * f[1]).astype(BF16)
    acc_ref[...] += jnp.dot(gact, wd_ref[...], preferred_element_type=F32)

    @pl.when(j == nj - 1)
    def _():
        y_ref[...] = acc_ref[...]


def _ffn_sample(x, g, buf, wu, wd, cw, cb, hc):
    m, d = x.shape
    nj = D_FF // hc
    buf4 = buf.reshape(m, FFN_CONV - 1, 2, D_FF).transpose(1, 2, 0, 3)
    wu3 = wu.reshape(d, 2, D_FF).transpose(1, 0, 2)
    cw3 = cw.reshape(FFN_CONV, 2, D_FF).transpose(1, 0, 2)
    cb3 = cb.reshape(2, 1, D_FF)
    bspec = pl.BlockSpec((FFN_CONV - 1, 2, m, hc), lambda j: (0, 0, 0, j))
    y, new = pl.pallas_call(
        _ffn_sample_kernel,
        grid=(nj,),
        in_specs=[pl.BlockSpec((m, d), lambda j: (0, 0)),
                  pl.BlockSpec((1, d), lambda j: (0, 0)),
                  bspec,
                  pl.BlockSpec((2, d, hc), lambda j: (0, 0, j)),
                  pl.BlockSpec((hc, d), lambda j: (j, 0)),
                  pl.BlockSpec((2, FFN_CONV, hc), lambda j: (0, 0, j)),
                  pl.BlockSpec((2, 1, hc), lambda j: (0, 0, j))],
        out_specs=[pl.BlockSpec((m, d), lambda j: (0, 0)), bspec],
        out_shape=[jax.ShapeDtypeStruct((m, d), F32),
                   jax.ShapeDtypeStruct((FFN_CONV - 1, 2, m, D_FF), F32)],
        scratch_shapes=[pltpu.VMEM((m, d), BF16), pltpu.VMEM((m, d), F32)],
        compiler_params=_cparams(("arbitrary",)),
        name="ffn_sample",
    )(x, g, buf4, wu3, wd, cw3, cb3)
    return y, new.transpose(2, 0, 1, 3).reshape(m, FFN_CONV - 1, 2 * D_FF)


def _prep_layer_weights(lw):
    w_in = lw['w_in']
    o1 = D_RNN
    o2 = o1 + N_GROUPS * 3 * ATT_W
    o3 = o2 + 2 * D_CONV
    w_perm = jnp.concatenate([w_in[:, o2:o3], w_in[:, :o1], w_in[:, o3:], w_in[:, o1:o2]], axis=1)
    row = lambda v: v.reshape(1, -1)
    return dict(
        norm1_g=row(lw['norm1_g']), w_in=w_perm.astype(BF16),
        rnn_conv_w=lw['rnn_conv_w'], rnn_conv_b=row(lw['rnn_conv_b']),
        rnn_wa=lw['rnn_wa'].astype(BF16), rnn_wx=lw['rnn_wx'].astype(BF16),
        rnn_ba=row(lw['rnn_ba']), rnn_bx=row(lw['rnn_bx']), rnn_lambda=row(lw['rnn_lambda']),
        q_norm_g=lw['q_norm_g'], k_norm_g=lw['k_norm_g'],
        cconv_w=lw['cconv_w'], cconv_b=row(lw['cconv_b']),
        cnorm_g=row(lw['cnorm_g']), cnorm_b=row(lw['cnorm_b']),
        w_br_rnn=lw['w_br_rnn'].astype(BF16), w_br_attn=lw['w_br_attn'].astype(BF16),
        w_br_conv=lw['w_br_conv'].astype(BF16), w_o=lw['w_o'].astype(BF16),
        norm2_g=row(lw['norm2_g']), ffn_up=lw['ffn_up'].astype(BF16),
        ffn_conv_w=lw['ffn_conv_w'], ffn_conv_b=row(lw['ffn_conv_b']),
        ffn_down=lw['ffn_down'].astype(BF16))


def _prompt_layer(x, w, tiles):
    n, s, d = x.shape
    t = n * s
    z_rnn, z_gate, zqkv, c, c_tail = _fused_inproj(
        x, w['norm1_g'], w['w_in'], w['cconv_w'], w['cconv_b'], w['cnorm_g'], w['cnorm_b'],
        tiles['in_tm'])
    zqkv = zqkv.reshape(N_GROUPS * 3, H_G, n, s, HD)
    h_seq, h_last = _rnn_prompt(z_rnn, w['rnn_conv_w'], w['rnn_conv_b'], w['rnn_wa'], w['rnn_wx'],
                                w['rnn_ba'], w['rnn_bx'], w['rnn_lambda'], tiles['rnn_tm'])
    o_list, lse_list, kv_new = [], [], []
    for gi, (win, dil) in enumerate(ATT_GROUPS):
        o, lse, kn = _attn_prompt(zqkv, w['q_norm_g'][gi:gi + 1], w['k_norm_g'][gi:gi + 1], gi,
                                  tiles['att_tt'])
        o_list.append(o.reshape(H_G, t, HD))
        lse_list.append(lse.reshape(t, HD))
        k_tail = kn.transpose(1, 2, 0, 3)
        v_tail = zqkv[3 * gi + 2, :, :, s - win:].transpose(1, 2, 0, 3)
        kv_new.append(jnp.stack([k_tail, v_tail], axis=2))
    x1 = _merge(x.reshape(t, d), z_gate.reshape(t, N_BRANCH * D_MODEL), 0,
                h_seq.reshape(t, D_RNN), o_list, lse_list,
                c.reshape(t, D_CONV), w['w_br_rnn'], w['w_br_attn'], w['w_br_conv'], w['w_o'],
                tiles['mrg_tm'], True)
    x2, ta, tb = _ffn_prompt(x1.reshape(n, s, d), w['norm2_g'], w['ffn_up'], w['ffn_down'],
                             w['ffn_conv_w'], w['ffn_conv_b'], tiles['ffn_tm'], tiles['ffn_hc'])
    rnn_buf_new = z_rnn[:, s - (RNN_CONV - 1):].astype(F32)
    cconv_new = c_tail[:, CC_PAD - (CCONV - 1):]
    ffn_new = jnp.concatenate([ta[:, -1, 8 - (FFN_CONV - 1):], tb[:, -1, 8 - (FFN_CONV - 1):]],
                              axis=-1)
    return x2, kv_new, h_last.reshape(n, D_RNN), rnn_buf_new, cconv_new, ffn_new


def _sample_layer(x, w, layer, rnn_h, rnn_buf, cconv_buf, ffn_buf, caches, tiles):
    m, _, d = x.shape
    x2d = x.reshape(m, d)
    z, zqkv = _inproj(x2d, w['norm1_g'], w['w_in'], m, tiles['in_tn'])
    h, c, rnew, cnew = _state_sample(
        z, rnn_h, rnn_buf, cconv_buf, w['rnn_conv_w'], w['rnn_conv_b'], w['rnn_wa'], w['rnn_wx'],
        w['rnn_ba'], w['rnn_bx'], w['rnn_lambda'], w['cconv_w'], w['cconv_b'],
        w['cnorm_g'], w['cnorm_b'], tiles['smp_state_nb'])
    qkv = zqkv.transpose(1, 0, 2).reshape(m, N_GROUPS, 3, H_G, HD)
    o_list, lse_list, kv_new = [], [], []
    for gi in range(N_GROUPS):
        o, lse, kvn = _attn_sample(qkv, caches[gi], layer, w['q_norm_g'][gi], w['k_norm_g'][gi],
                                   gi, tiles['smp_attn_nb'])
        o_list.append(o.transpose(1, 0, 2))
        lse_list.append(jnp.pad(lse[:, :, 0], ((0, 0), (0, HD - H_G))))
        kv_new.append(kvn.reshape(m, 1, 2, H_G, HD))
    x1 = _merge(x2d, z, COL_GATE // (N_BRANCH * D_MODEL), h.astype(BF16), o_list, lse_list, c,
                w['w_br_rnn'], w['w_br_attn'], w['w_br_conv'], w['w_o'], m, False)
    x2, ffn_new = _ffn_sample(x1, w['norm2_g'], ffn_buf, w['ffn_up'], w['ffn_down'],
                              w['ffn_conv_w'], w['ffn_conv_b'], tiles['ffn_hc'])
    return x2.reshape(m, 1, d), kv_new, h, rnew, cnew, ffn_new


def _tiles(s):
    return dict(in_tm=min(1024, s), in_tn=1536, rnn_tm=min(1024, s), att_tt=2048,
                mrg_tm=min(512, s), ffn_tm=min(1024, s), ffn_hc=1024,
                smp_state_nb=32, smp_attn_nb=8)


def kernel(x_prompt, x_sample, cache_kv_w128, cache_kv_w512, cache_kv_w2048, state_rnn_h, state_rnn_conv, state_cconv, state_ffn_conv, norm1_g, w_in, rnn_conv_w, rnn_conv_b, rnn_wa, rnn_ba, rnn_wx, rnn_bx, rnn_lambda, q_norm_g, k_norm_g, cconv_w, cconv_b, cnorm_g, cnorm_b, w_br_rnn, w_br_attn, w_br_conv, w_o, norm2_g, ffn_up, ffn_conv_w, ffn_conv_b, ffn_down):
    params = dict(norm1_g=norm1_g, w_in=w_in, rnn_conv_w=rnn_conv_w, rnn_conv_b=rnn_conv_b,
                  rnn_wa=rnn_wa, rnn_ba=rnn_ba, rnn_wx=rnn_wx, rnn_bx=rnn_bx,
                  rnn_lambda=rnn_lambda, q_norm_g=q_norm_g, k_norm_g=k_norm_g, cconv_w=cconv_w,
                  cconv_b=cconv_b, cnorm_g=cnorm_g, cnorm_b=cnorm_b, w_br_rnn=w_br_rnn,
                  w_br_attn=w_br_attn, w_br_conv=w_br_conv, w_o=w_o, norm2_g=norm2_g,
                  ffn_up=ffn_up, ffn_conv_w=ffn_conv_w, ffn_conv_b=ffn_conv_b, ffn_down=ffn_down)
    depth = w_in.shape[0]
    assert x_sample.shape[1] == 1
    caches = (cache_kv_w128, cache_kv_w512, cache_kv_w2048)
    tiles = _tiles(x_prompt.shape[1])
    xp, xs = x_prompt, x_sample
    pk, sk = [[], [], []], [[], [], []]
    p_state = [[], [], [], []]
    s_state = [[], [], [], []]
    for l in range(depth):
        w = _prep_layer_weights({k: v[l] for k, v in params.items()})
        xp, kvp, hp, rbp, cbp, fbp = _prompt_layer(xp, w, tiles)
        xs, kvs, hs, rbs, cbs, fbs = _sample_layer(
            xs, w, l, state_rnn_h[l], state_rnn_conv[l], state_cconv[l], state_ffn_conv[l],
            caches, tiles)
        for gi in range(N_GROUPS):
            pk[gi].append(kvp[gi])
            sk[gi].append(kvs[gi])
        for lst, v in zip(p_state, (hp, rbp, cbp, fbp)):
            lst.append(v)
        for lst, v in zip(s_state, (hs, rbs, cbs, fbs)):
            lst.append(v)
    stack = lambda xs_: jnp.stack(xs_, 0)
    return (xp, xs, stack(pk[0]), stack(pk[1]), stack(pk[2]),
            stack(p_state[0]), stack(p_state[1]), stack(p_state[2]), stack(p_state[3]),
            stack(sk[0]), stack(sk[1]), stack(sk[2]),
            stack(s_state[0]), stack(s_state[1]), stack(s_state[2]), stack(s_state[3]))
```
